```python
import math
import jax, jax.numpy as jnp
from jax import lax
import numpy as np

D_MODEL = 1024
BATCH = 1
SEQ = 16384
DEPTH = 1

HEAD_DIM = 64
FOX_HEADS = 8
FOX_WIDTH = FOX_HEADS * HEAD_DIM
DIFF_HEADS = 4
DIFF_V_DIM = 2 * HEAD_DIM
DIFF_WIDTH = DIFF_HEADS * DIFF_V_DIM
DIFF_QK_WIDTH = DIFF_HEADS * 2 * HEAD_DIM
MIX_WIDTH = FOX_WIDTH + DIFF_WIDTH
D_FF = 4 * D_MODEL
Q_BLOCK = 128
RMS_EPS = 1e-6
FORGET_BIAS_INIT = 3.0
LAMBDA_INIT_STD = 0.1
IN_SPLITS = (FOX_WIDTH, FOX_WIDTH, FOX_WIDTH, FOX_HEADS, DIFF_QK_WIDTH, DIFF_QK_WIDTH, DIFF_WIDTH)
IN_COLS = sum(IN_SPLITS)

kernel_name = "hybrid_fox_diffattn_block"


def rmsnorm(x, g):
    xf = x.astype(jnp.float32)
    y = xf * lax.rsqrt(jnp.mean(xf * xf, axis=-1, keepdims=True) + RMS_EPS)
    return (y * g.astype(jnp.float32)).astype(x.dtype)


def lambda_init_fn(layer_idx):
    return 0.8 - 0.6 * math.exp(-0.3 * layer_idx)


def alibi_slopes(n_heads):
    return jnp.asarray(2.0 ** (-8.0 * np.arange(1, n_heads + 1) / n_heads), dtype=jnp.float32)


def sweep_query_blocks(step, seq):
    n_blocks = seq // Q_BLOCK
    out = lax.map(step, jnp.arange(n_blocks))
    out = jnp.moveaxis(out, 0, 1)
    return out.reshape(out.shape[0], seq, out.shape[-1])


def causal_positions(i, seq):
    q_pos = i * Q_BLOCK + jnp.arange(Q_BLOCK)
    k_pos = jnp.arange(seq)
    return q_pos, k_pos, k_pos[None, :] <= q_pos[:, None]


def forgetting_attention(q, k, v, f_logit):
    B, S, H, Dh = q.shape
    scale = Dh ** -0.5
    c = jnp.cumsum(jax.nn.log_sigmoid(f_logit.astype(jnp.float32)), axis=1)
    c = jnp.transpose(c, (0, 2, 1))

    def step(i):
        start = i * Q_BLOCK
        q_i = lax.dynamic_slice_in_dim(q, start, Q_BLOCK, axis=1)
        c_i = lax.dynamic_slice_in_dim(c, start, Q_BLOCK, axis=2)
        _, _, mask = causal_positions(i, S)
        s = jnp.einsum('bqhd,bkhd->bhqk', q_i, k).astype(jnp.float32) * scale
        s = s + c_i[..., :, None] - c[..., None, :]
        s = jnp.where(mask, s, -jnp.inf)
        p = jax.nn.softmax(s, axis=-1)
        o = jnp.einsum('bhqk,bkhd->bqhd', p.astype(v.dtype), v)
        return o.reshape(B, Q_BLOCK, H * Dh)

    return sweep_query_blocks(step, S)


def differential_attention(q, k, v, lam, g_sub, lam_init):
    B, S, H, _, Dh = q.shape
    Dv = v.shape[-1]
    scale = Dh ** -0.5
    slopes = alibi_slopes(H)

    def step(i):
        start = i * Q_BLOCK
        q_i = lax.dynamic_slice_in_dim(q, start, Q_BLOCK, axis=1)
        q_pos, k_pos, mask = causal_positions(i, S)
        dist = (q_pos[:, None] - k_pos[None, :]).astype(jnp.float32)
        bias = -slopes[:, None, None] * dist
        s = jnp.einsum('bqhnd,bkhnd->bhnqk', q_i, k).astype(jnp.float32) * scale
        s = s + bias[None, :, None]
        s = jnp.where(mask, s, -jnp.inf)
        p = jax.nn.softmax(s, axis=-1)
        a = p[:, :, 0] - lam * p[:, :, 1]
        o = jnp.einsum('bhqk,bkhd->bqhd', a.astype(v.dtype), v)
        o = rmsnorm(o, g_sub) * (1.0 - lam_init)
        return o.reshape(B, Q_BLOCK, H * Dv)

    return sweep_query_blocks(step, S)


def setup_inputs(seed: int = 0) -> dict:
    key = jax.random.key(seed)
    ks = jax.random.split(key, 16)
    f32 = jnp.float32

    def gain(k, n):
        return 1.0 + 0.02 * jax.random.normal(k, (DEPTH, n), f32)

    return {
        "x": jax.random.normal(ks[0], (BATCH, SEQ, D_MODEL), f32),
        "g_pre_mix": gain(ks[1], D_MODEL),
        "w_in": jax.random.normal(ks[2], (DEPTH, D_MODEL, IN_COLS), f32) * D_MODEL ** -0.5,
        "b_forget": FORGET_BIAS_INIT + 0.1 * jax.random.normal(ks[3], (DEPTH, FOX_HEADS), f32),
        "lambda_q1": LAMBDA_INIT_STD * jax.random.normal(ks[4], (DEPTH, HEAD_DIM), f32),
        "lambda_k1": LAMBDA_INIT_STD * jax.random.normal(ks[5], (DEPTH, HEAD_DIM), f32),
        "lambda_q2": LAMBDA_INIT_STD * jax.random.normal(ks[6], (DEPTH, HEAD_DIM), f32),
        "lambda_k2": LAMBDA_INIT_STD * jax.random.normal(ks[7], (DEPTH, HEAD_DIM), f32),
        "g_subln": gain(ks[8], DIFF_V_DIM),
        "w_out": jax.random.normal(ks[9], (DEPTH, MIX_WIDTH, D_MODEL), f32) * MIX_WIDTH ** -0.5,
        "g_post_mix": gain(ks[10], D_MODEL),
        "g_pre_mlp": gain(ks[11], D_MODEL),
        "w_up": jax.random.normal(ks[12], (DEPTH, D_MODEL, D_FF), f32) * D_MODEL ** -0.5,
        "w_down": jax.random.normal(ks[13], (DEPTH, D_FF, D_MODEL), f32) * D_FF ** -0.5,
        "g_post_mlp": gain(ks[14], D_MODEL),
    }


def reference(x, g_pre_mix, w_in, b_forget, lambda_q1, lambda_k1, lambda_q2, lambda_k2,
              g_subln, w_out, g_post_mix, g_pre_mlp, w_up, w_down, g_post_mlp):
    B, S, _ = x.shape
    split_idx = [int(v) for v in np.cumsum(IN_SPLITS)[:-1]]
    for l in range(DEPTH):
        h = rmsnorm(x, g_pre_mix[l])
        proj = jnp.einsum('bsd,dc->bsc', h, w_in[l])
        fq, fk, fv, f_logit, dq, dk, dv = jnp.split(proj, split_idx, axis=-1)
        f_logit = f_logit + b_forget[l]
        fox = forgetting_attention(
            fq.reshape(B, S, FOX_HEADS, HEAD_DIM),
            fk.reshape(B, S, FOX_HEADS, HEAD_DIM),
            fv.reshape(B, S, FOX_HEADS, HEAD_DIM),
            f_logit)
        lam_init = lambda_init_fn(l)
        lam = (jnp.exp(jnp.sum(lambda_q1[l].astype(jnp.float32) * lambda_k1[l].astype(jnp.float32)))
               - jnp.exp(jnp.sum(lambda_q2[l].astype(jnp.float32) * lambda_k2[l].astype(jnp.float32)))
               + lam_init)
        diff = differential_attention(
            dq.reshape(B, S, DIFF_HEADS, 2, HEAD_DIM),
            dk.reshape(B, S, DIFF_HEADS, 2, HEAD_DIM),
            dv.reshape(B, S, DIFF_HEADS, DIFF_V_DIM),
            lam, g_subln[l], lam_init)
        mix = jnp.einsum('bsc,cd->bsd', jnp.concatenate([fox, diff], axis=-1), w_out[l])
        x = x + rmsnorm(mix, g_post_mix[l])
        h = rmsnorm(x, g_pre_mlp[l])
        u = jnp.square(jax.nn.relu(jnp.einsum('bsd,df->bsf', h, w_up[l])))
        x = x + rmsnorm(jnp.einsum('bsf,fd->bsd', u, w_down[l]), g_post_mlp[l])
    return x
```

```python
import functools
import math

import numpy as np
import jax
import jax.numpy as jnp
from jax import lax
from jax.experimental import pallas as pl
from jax.experimental.pallas import tpu as pltpu

D_MODEL = 1024
HEAD_DIM = 64
FOX_HEADS = 8
FOX_WIDTH = FOX_HEADS * HEAD_DIM
DIFF_HEADS = 4
DIFF_MAPS = 2 * DIFF_HEADS
DIFF_V_DIM = 2 * HEAD_DIM
DIFF_WIDTH = DIFF_HEADS * DIFF_V_DIM
D_FF = 4 * D_MODEL
RMS_EPS = 1e-6
LAM_INIT = 0.8 - 0.6 * math.exp(-0.3 * 0)
SCORE_SCALE = HEAD_DIM ** -0.5

QK_FEATS = 128
BIAS_ROWS = 16
FOX_V_ROWS = HEAD_DIM + BIAS_ROWS
DIFF_V_ROWS = DIFF_V_DIM + BIAS_ROWS
MASK_VALUE = -1e30

PROJ_ROWS = 512
ATT_BQ = 512
ATT_BK = 512
OUT_ROWS = 512
FF_CHUNK = 1024
VMEM_LIMIT_BYTES = 56 * 1024 * 1024

_F32 = jnp.float32
_BF16 = jnp.bfloat16


def _split3(v):
    hi = v.astype(_BF16).astype(_F32)
    r = v - hi
    mid = r.astype(_BF16).astype(_F32)
    lo = (r - mid).astype(_BF16).astype(_F32)
    return hi, mid, lo


def _bias_selectors():
    gq = np.zeros((8 * BIAS_ROWS, 32), np.float32)
    gk = np.zeros((8 * BIAS_ROWS, 32), np.float32)
    for h in range(8):
        for part in range(3):
            gq[h * BIAS_ROWS + part, part * 8 + h] = 1.0
            gq[h * BIAS_ROWS + 3 + part, 24 + h] = 1.0
            gk[h * BIAS_ROWS + part, 24 + h] = 1.0
            gk[h * BIAS_ROWS + 3 + part, part * 8 + h] = -1.0
    return jnp.asarray(gq, _BF16), jnp.asarray(gk, _BF16)


def _rms_scale(v, axis):
    return lax.rsqrt(jnp.mean(v * v, axis=axis, keepdims=True) + RMS_EPS)


def _proj_kernel(x_ref, g_ref, w_ref, bf_ref, slope_ref, gq_ref, gk_ref,
                 fq_ref, fk_ref, fv_ref, dq_ref, dk_ref, dv_ref, carry_ref):
    t = pl.program_id(0)
    rows = x_ref.shape[0]

    @pl.when(t == 0)
    def _():
        carry_ref[...] = jnp.zeros_like(carry_ref)

    x = x_ref[...]
    h = ((x * _rms_scale(x, -1)) * g_ref[...]).astype(_BF16)

    def proj_t(lo, hi):
        return lax.dot_general(w_ref[lo:hi, :], h, (((1,), (1,)), ((), ())),
                               preferred_element_type=_F32)

    ones8 = jnp.ones((8, rows), _F32)

    def bias_feats(c):
        c_hi, c_mid, c_lo = _split3(c)
        c3 = jnp.concatenate([c_hi, c_mid, c_lo, ones8], axis=0).astype(_BF16)
        fq = jnp.dot(gq_ref[...], c3, preferred_element_type=_F32)
        fk = jnp.dot(gk_ref[...], c3, preferred_element_type=_F32)
        return fq, fk

    fl = proj_t(3 * FOX_WIDTH + 3 * DIFF_WIDTH, 3 * FOX_WIDTH + 3 * DIFF_WIDTH + 16)[0:8]
    fl = fl + bf_ref[...]
    ls = jnp.minimum(fl, 0.0) - jnp.log1p(jnp.exp(-jnp.abs(fl)))
    l_hi, l_mid, l_lo = _split3(ls)
    stack = jnp.concatenate([l_hi, l_mid, l_lo, jnp.zeros((8, rows), _F32)], axis=0).astype(_BF16)
    r_i = lax.broadcasted_iota(jnp.int32, (rows, rows), 0)
    c_i = lax.broadcasted_iota(jnp.int32, (rows, rows), 1)
    upper = (r_i <= c_i).astype(_BF16)
    cs = jnp.dot(stack, upper, preferred_element_type=_F32)
    c_fox = carry_ref[:, 0:1] + ((cs[0:8] + cs[8:16]) + cs[16:24])
    carry_ref[...] = jnp.broadcast_to(c_fox[:, rows - 1:rows], carry_ref.shape)

    pos = (lax.broadcasted_iota(jnp.int32, (8, rows), 1) + t * rows).astype(_F32)
    c_alibi = -(slope_ref[...] * pos)

    ones_row = (lax.broadcasted_iota(jnp.int32, (BIAS_ROWS, rows), 0) == 0).astype(_BF16)
    zero_pad_q = jnp.zeros((QK_FEATS - HEAD_DIM - BIAS_ROWS, rows), _BF16)
    zero_pad_k = jnp.zeros((QK_FEATS - HEAD_DIM - BIAS_ROWS, rows), _F32)
    vb = fv_ref.shape[-1]

    def emit_qk(q_t, k_t, c, q_out, k_out):
        feat_q, feat_k = bias_feats(c)
        for m in range(8):
            d0, b0 = m * HEAD_DIM, m * BIAS_ROWS
            q_out[m, 0:HEAD_DIM, :] = (q_t[d0:d0 + HEAD_DIM] * SCORE_SCALE).astype(_BF16)
            q_out[m, HEAD_DIM:HEAD_DIM + BIAS_ROWS, :] = feat_q[b0:b0 + BIAS_ROWS].astype(_BF16)
            q_out[m, HEAD_DIM + BIAS_ROWS:, :] = zero_pad_q
            k_aug = jnp.concatenate(
                [k_t[d0:d0 + HEAD_DIM], feat_k[b0:b0 + BIAS_ROWS], zero_pad_k], axis=0)
            k_out[m] = k_aug.T.astype(_BF16)

    def emit_v(v_t, n_heads, v_dim, v_out):
        v_bf = v_t.astype(_BF16)
        for hd in range(n_heads):
            for b in range(rows // vb):
                v_out[hd, b, 0:v_dim, :] = v_bf[hd * v_dim:(hd + 1) * v_dim, b * vb:(b + 1) * vb]
                v_out[hd, b, v_dim:, :] = ones_row[:, b * vb:(b + 1) * vb]

    o = 0
    fq_t = proj_t(o, o + FOX_WIDTH); o += FOX_WIDTH
    fk_t = proj_t(o, o + FOX_WIDTH); o += FOX_WIDTH
    emit_qk(fq_t, fk_t, c_fox, fq_ref, fk_ref)
    fv_t = proj_t(o, o + FOX_WIDTH); o += FOX_WIDTH
    emit_v(fv_t, FOX_HEADS, HEAD_DIM, fv_ref)
    dq_t = proj_t(o, o + DIFF_WIDTH); o += DIFF_WIDTH
    dk_t = proj_t(o, o + DIFF_WIDTH); o += DIFF_WIDTH
    emit_qk(dq_t, dk_t, c_alibi, dq_ref, dk_ref)
    dv_t = proj_t(o, o + DIFF_WIDTH); o += DIFF_WIDTH
    emit_v(dv_t, DIFF_HEADS, DIFF_V_DIM, dv_ref)


def _projection(x2, g_pre_mix, w_in, b_forget):
    seq = x2.shape[0]
    rows = PROJ_ROWS
    n_t = seq // rows
    nvb = seq // ATT_BK
    vb_per_t = rows // ATT_BK
    f0 = 3 * FOX_WIDTH
    w_main = jnp.concatenate([w_in[:, :f0], w_in[:, f0 + FOX_HEADS:]], axis=1)
    w_f = jnp.pad(w_in[:, f0:f0 + FOX_HEADS], ((0, 0), (0, 16 - FOX_HEADS)))
    w_t = jnp.concatenate([w_main, w_f], axis=1).T.astype(_BF16)
    n_feat = w_t.shape[0]
    slopes = 2.0 ** (-8.0 * np.arange(1, DIFF_HEADS + 1) / DIFF_HEADS)
    slope_col = jnp.asarray(np.repeat(slopes, 2).reshape(DIFF_MAPS, 1), _F32)
    gq, gk = _bias_selectors()

    const2 = lambda t: (0, 0)
    out_shapes = (
        jax.ShapeDtypeStruct((FOX_HEADS, QK_FEATS, seq), _BF16),
        jax.ShapeDtypeStruct((FOX_HEADS, seq, QK_FEATS), _BF16),
        jax.ShapeDtypeStruct((FOX_HEADS, nvb, FOX_V_ROWS, ATT_BK), _BF16),
        jax.ShapeDtypeStruct((DIFF_MAPS, QK_FEATS, seq), _BF16),
        jax.ShapeDtypeStruct((DIFF_MAPS, seq, QK_FEATS), _BF16),
        jax.ShapeDtypeStruct((DIFF_HEADS, nvb, DIFF_V_ROWS, ATT_BK), _BF16),
    )
    out_specs = (
        pl.BlockSpec((FOX_HEADS, QK_FEATS, rows), lambda t: (0, 0, t)),
        pl.BlockSpec((FOX_HEADS, rows, QK_FEATS), lambda t: (0, t, 0)),
        pl.BlockSpec((FOX_HEADS, vb_per_t, FOX_V_ROWS, ATT_BK), lambda t: (0, t, 0, 0)),
        pl.BlockSpec((DIFF_MAPS, QK_FEATS, rows), lambda t: (0, 0, t)),
        pl.BlockSpec((DIFF_MAPS, rows, QK_FEATS), lambda t: (0, t, 0)),
        pl.BlockSpec((DIFF_HEADS, vb_per_t, DIFF_V_ROWS, ATT_BK), lambda t: (0, t, 0, 0)),
    )
    return pl.pallas_call(
        _proj_kernel,
        grid=(n_t,),
        in_specs=[
            pl.BlockSpec((rows, D_MODEL), lambda t: (t, 0)),
            pl.BlockSpec((1, D_MODEL), const2),
            pl.BlockSpec((n_feat, D_MODEL), const2),
            pl.BlockSpec((FOX_HEADS, 1), const2),
            pl.BlockSpec((DIFF_MAPS, 1), const2),
            pl.BlockSpec(gq.shape, const2),
            pl.BlockSpec(gk.shape, const2),
        ],
        out_specs=out_specs,
        out_shape=out_shapes,
        scratch_shapes=[pltpu.VMEM((FOX_HEADS, 128), _F32)],
        compiler_params=pltpu.CompilerParams(
            dimension_semantics=("arbitrary",), vmem_limit_bytes=VMEM_LIMIT_BYTES),
        name="proj",
    )(x2, g_pre_mix.reshape(1, D_MODEL), w_t, b_forget.reshape(FOX_HEADS, 1), slope_col, gq, gk)


def _flash_t(q_t, k_ref, k_idx, v_ref, v_idx, qi, v_rows):
    bq = q_t.shape[1]
    bk = ATT_BK

    def step(j, carry, masked):
        m, acc = carry
        start = pl.multiple_of(j * bk, bk)
        k_blk = k_ref[k_idx, pl.ds(start, bk), :]
        s_t = jnp.dot(k_blk, q_t, preferred_element_type=_F32)
        if masked:
            key = lax.broadcasted_iota(jnp.int32, (bk, bq), 0) + j * bk
            qry = lax.broadcasted_iota(jnp.int32, (bk, bq), 1) + qi * bq
            s_t = jnp.where(key <= qry, s_t, MASK_VALUE)
        m_new = jnp.maximum(m, jnp.max(s_t, axis=0, keepdims=True))
        alpha = jnp.exp(m - m_new)
        p_t = jnp.exp(s_t - m_new).astype(_BF16)
        v_blk = v_ref[v_idx, j]
        acc = alpha * acc + jnp.dot(v_blk, p_t, preferred_element_type=_F32)
        return m_new, acc

    carry = (jnp.full((1, bq), MASK_VALUE, _F32), jnp.zeros((v_rows, bq), _F32))
    n_full = qi * (bq // bk)
    carry = lax.fori_loop(0, n_full, functools.partial(step, masked=False), carry)
    for d in range(bq // bk):
        carry = step(n_full + d, carry, True)
    return carry[1]


def _fox_kernel(q_ref, k_ref, v_ref, o_ref):
    qi = pl.program_id(1)
    outs = []
    for hh in range(2):
        acc = _flash_t(q_ref[hh], k_ref, hh, v_ref, hh, qi, FOX_V_ROWS)
        outs.append(acc[0:HEAD_DIM] / acc[HEAD_DIM:HEAD_DIM + 1])
    o_t = jnp.concatenate(outs, axis=0)
    o_ref[...] = o_t.T.astype(o_ref.dtype)


def _diff_kernel(q_ref, k_ref, v_ref, lam_ref, g_ref, o_ref):
    qi = pl.program_id(1)
    lp = lam_ref[...]
    lam = (jnp.exp(jnp.sum(lp[0:1] * lp[1:2], axis=1, keepdims=True))
           - jnp.exp(jnp.sum(lp[2:3] * lp[3:4], axis=1, keepdims=True)) + LAM_INIT)
    outs = []
    for n in range(2):
        acc = _flash_t(q_ref[n], k_ref, n, v_ref, 0, qi, DIFF_V_ROWS)
        outs.append(acc[0:DIFF_V_DIM] / acc[DIFF_V_DIM:DIFF_V_DIM + 1])
    o_t = outs[0] - lam * outs[1]
    o_t = ((o_t * _rms_scale(o_t, 0)) * g_ref[...]) * (1.0 - LAM_INIT)
    o_ref[...] = o_t.T.astype(o_ref.dtype)


def _attention(q_t, k, v_t, kernel_fn, maps_per_step, v_per_step, extra=(), name=None):
    n_maps, _, seq = q_t.shape
    nvb, v_rows = v_t.shape[1], v_t.shape[2]
    n_groups = n_maps // maps_per_step
    extra_specs = [pl.BlockSpec(e.shape, lambda g, qi: (0, 0)) for e in extra]
    return pl.pallas_call(
        kernel_fn,
        grid=(n_groups, seq // ATT_BQ),
        in_specs=[
            pl.BlockSpec((maps_per_step, QK_FEATS, ATT_BQ), lambda g, qi: (g, 0, qi)),
            pl.BlockSpec((maps_per_step, seq, QK_FEATS), lambda g, qi: (g, 0, 0)),
            pl.BlockSpec((v_per_step, nvb, v_rows, ATT_BK), lambda g, qi: (g, 0, 0, 0)),
        ] + extra_specs,
        out_specs=pl.BlockSpec((ATT_BQ, 128), lambda g, qi: (qi, g)),
        out_shape=jax.ShapeDtypeStruct((seq, n_groups * 128), _BF16),
        compiler_params=pltpu.CompilerParams(
            dimension_semantics=("arbitrary", "arbitrary"), vmem_limit_bytes=VMEM_LIMIT_BYTES),
        name=name,
    )(q_t, k, v_t, *extra)


def _out_kernel(x_ref, fox_ref, diff_ref, wo_ref, wu_ref, wd_ref, g_ref, o_ref):
    g = g_ref[...]
    mix = (jnp.dot(fox_ref[...], wo_ref[0:FOX_WIDTH, :], preferred_element_type=_F32)
           + jnp.dot(diff_ref[...], wo_ref[FOX_WIDTH:, :], preferred_element_type=_F32))
    x1 = x_ref[...] + (mix * _rms_scale(mix, -1)) * g[0:1]
    h = ((x1 * _rms_scale(x1, -1)) * g[1:2]).astype(_BF16)
    d = jnp.zeros_like(x1)
    for f in range(D_FF // FF_CHUNK):
        u = jnp.dot(h, wu_ref[:, f * FF_CHUNK:(f + 1) * FF_CHUNK], preferred_element_type=_F32)
        u = jnp.square(jnp.maximum(u, 0.0)).astype(_BF16)
        d = d + jnp.dot(u, wd_ref[f * FF_CHUNK:(f + 1) * FF_CHUNK, :], preferred_element_type=_F32)
    o_ref[...] = x1 + (d * _rms_scale(d, -1)) * g[2:3]


def _output_stage(x2, fox, diff, w_out, w_up, w_down, gains):
    seq = x2.shape[0]
    rows = OUT_ROWS
    const2 = lambda t: (0, 0)
    resident = functools.partial(pl.BlockSpec, index_map=const2, pipeline_mode=pl.Buffered(1))
    return pl.pallas_call(
        _out_kernel,
        grid=(seq // rows,),
        in_specs=[
            pl.BlockSpec((rows, D_MODEL), lambda t: (t, 0)),
            pl.BlockSpec((rows, FOX_WIDTH), lambda t: (t, 0)),
            pl.BlockSpec((rows, DIFF_WIDTH), lambda t: (t, 0)),
            resident((D_MODEL, D_MODEL)),
            resident((D_MODEL, D_FF)),
            resident((D_FF, D_MODEL)),
            pl.BlockSpec((3, D_MODEL), const2),
        ],
        out_specs=pl.BlockSpec((rows, D_MODEL), lambda t: (t, 0)),
        out_shape=jax.ShapeDtypeStruct((seq, D_MODEL), _F32),
        compiler_params=pltpu.CompilerParams(
            dimension_semantics=("arbitrary",), vmem_limit_bytes=VMEM_LIMIT_BYTES),
        name="out_mlp",
    )(x2, fox, diff, w_out, w_up, w_down, gains)


def kernel(x, g_pre_mix, w_in, b_forget, lambda_q1, lambda_k1, lambda_q2, lambda_k2,
           g_subln, w_out, g_post_mix, g_pre_mlp, w_up, w_down, g_post_mlp):
    batch, seq, _ = x.shape
    assert batch == 1 and w_in.shape[0] == 1
    assert seq % PROJ_ROWS == 0 and seq % ATT_BQ == 0 and seq % OUT_ROWS == 0
    assert ATT_BQ % ATT_BK == 0 and PROJ_ROWS % ATT_BK == 0
    x2 = x.reshape(seq, D_MODEL)
    fq_t, fk, fv_t, dq_t, dk, dv_t = _projection(x2, g_pre_mix[0], w_in[0], b_forget[0])
    fox = _attention(fq_t, fk, fv_t, _fox_kernel, 2, 2, name="fox_attn")
    lam_params = jnp.stack([lambda_q1[0], lambda_k1[0], lambda_q2[0], lambda_k2[0]]).astype(_F32)
    diff = _attention(dq_t, dk, dv_t, _diff_kernel, 2, 1,
                      extra=(lam_params, g_subln[0].reshape(DIFF_V_DIM, 1)), name="diff_attn")
    gains = jnp.stack([g_post_mix[0], g_pre_mlp[0], g_post_mlp[0]])
    out = _output_stage(x2, fox, diff, w_out[0].astype(_BF16), w_up[0].astype(_BF16),
                        w_down[0].astype(_BF16), gains)
    return out.reshape(batch, seq, D_MODEL)
```

```python
import functools
import math

import numpy as np
import jax
import jax.numpy as jnp
from jax import lax
from jax.experimental import pallas as pl
from jax.experimental.pallas import tpu as pltpu

D_MODEL = 1024
HEAD_DIM = 64
FOX_HEADS = 8
FOX_WIDTH = FOX_HEADS * HEAD_DIM
DIFF_HEADS = 4
DIFF_MAPS = 2 * DIFF_HEADS
DIFF_V_DIM = 2 * HEAD_DIM
DIFF_WIDTH = DIFF_HEADS * DIFF_V_DIM
D_FF = 4 * D_MODEL
RMS_EPS = 1e-6
LAM_INIT = 0.8 - 0.6 * math.exp(-0.3 * 0)
SCORE_SCALE = HEAD_DIM ** -0.5
LOG2E = math.log2(math.e)

QK_FEATS = 128
BIAS_ROWS = 16
FOX_V_ROWS = HEAD_DIM + BIAS_ROWS
DIFF_V_ROWS = DIFF_V_DIM + BIAS_ROWS
MASK_VALUE = -1e30

PROJ_ROWS = 512
ATT_BQ = 512
ATT_BK = 512
MAX_CHUNK = 128
OUT_ROWS = 512
FF_CHUNK = 1024
VMEM_LIMIT_BYTES = 56 * 1024 * 1024

_F32 = jnp.float32
_BF16 = jnp.bfloat16


def _split3(v):
    hi = v.astype(_BF16).astype(_F32)
    r = v - hi
    mid = r.astype(_BF16).astype(_F32)
    lo = (r - mid).astype(_BF16).astype(_F32)
    return hi, mid, lo


def _bias_selectors():
    gq = np.zeros((8 * BIAS_ROWS, 32), np.float32)
    gk = np.zeros((8 * BIAS_ROWS, 32), np.float32)
    for h in range(8):
        for part in range(3):
            gq[h * BIAS_ROWS + part, part * 8 + h] = 1.0
            gq[h * BIAS_ROWS + 3 + part, 24 + h] = 1.0
            gk[h * BIAS_ROWS + part, 24 + h] = 1.0
            gk[h * BIAS_ROWS + 3 + part, part * 8 + h] = -1.0
    return jnp.asarray(gq, _BF16), jnp.asarray(gk, _BF16)


def _rms_scale(v, axis):
    return lax.rsqrt(jnp.mean(v * v, axis=axis, keepdims=True) + RMS_EPS)


def _proj_kernel(x_ref, g_ref, w_ref, bf_ref, slope_ref, gq_ref, gk_ref,
                 fq_ref, fk_ref, fv_ref, dq_ref, dk_ref, dv_ref, carry_ref):
    t = pl.program_id(0)
    rows = x_ref.shape[0]

    @pl.when(t == 0)
    def _():
        carry_ref[...] = jnp.zeros_like(carry_ref)

    x = x_ref[...]
    h = ((x * _rms_scale(x, -1)) * g_ref[...]).astype(_BF16)

    def proj_t(lo, hi):
        return lax.dot_general(w_ref[lo:hi, :], h, (((1,), (1,)), ((), ())),
                               preferred_element_type=_F32)

    ones8 = jnp.ones((8, rows), _F32)

    def bias_feats(c):
        c_hi, c_mid, c_lo = _split3(c * LOG2E)
        c3 = jnp.concatenate([c_hi, c_mid, c_lo, ones8], axis=0).astype(_BF16)
        fq = jnp.dot(gq_ref[...], c3, preferred_element_type=_F32)
        fk = jnp.dot(gk_ref[...], c3, preferred_element_type=_F32)
        return fq, fk

    fl = proj_t(3 * FOX_WIDTH + 3 * DIFF_WIDTH, 3 * FOX_WIDTH + 3 * DIFF_WIDTH + 16)[0:8]
    fl = fl + bf_ref[...]
    ls = jnp.minimum(fl, 0.0) - jnp.log1p(jnp.exp(-jnp.abs(fl)))
    l_hi, l_mid, l_lo = _split3(ls)
    stack = jnp.concatenate([l_hi, l_mid, l_lo, jnp.zeros((8, rows), _F32)], axis=0).astype(_BF16)
    r_i = lax.broadcasted_iota(jnp.int32, (rows, rows), 0)
    c_i = lax.broadcasted_iota(jnp.int32, (rows, rows), 1)
    upper = (r_i <= c_i).astype(_BF16)
    cs = jnp.dot(stack, upper, preferred_element_type=_F32)
    c_fox = carry_ref[:, 0:1] + ((cs[0:8] + cs[8:16]) + cs[16:24])
    carry_ref[...] = jnp.broadcast_to(c_fox[:, rows - 1:rows], carry_ref.shape)

    pos = (lax.broadcasted_iota(jnp.int32, (8, rows), 1) + t * rows).astype(_F32)
    c_alibi = -(slope_ref[...] * pos)

    ones_row = (lax.broadcasted_iota(jnp.int32, (BIAS_ROWS, rows), 0) == 0).astype(_BF16)
    zero_pad_q = jnp.zeros((QK_FEATS - HEAD_DIM - BIAS_ROWS, rows), _BF16)
    zero_pad_k = jnp.zeros((QK_FEATS - HEAD_DIM - BIAS_ROWS, rows), _F32)
    vb = fv_ref.shape[-1]

    def emit_qk(q_t, k_t, c, q_out, k_out):
        feat_q, feat_k = bias_feats(c)
        for m in range(8):
            d0, b0 = m * HEAD_DIM, m * BIAS_ROWS
            q_out[m, 0:HEAD_DIM, :] = (q_t[d0:d0 + HEAD_DIM] * (SCORE_SCALE * LOG2E)).astype(_BF16)
            q_out[m, HEAD_DIM:HEAD_DIM + BIAS_ROWS, :] = feat_q[b0:b0 + BIAS_ROWS].astype(_BF16)
            q_out[m, HEAD_DIM + BIAS_ROWS:, :] = zero_pad_q
            k_aug = jnp.concatenate(
                [k_t[d0:d0 + HEAD_DIM], feat_k[b0:b0 + BIAS_ROWS], zero_pad_k], axis=0)
            k_out[m] = k_aug.T.astype(_BF16)

    def emit_v(v_t, n_heads, v_dim, v_out):
        v_bf = v_t.astype(_BF16)
        for hd in range(n_heads):
            for b in range(rows // vb):
                v_out[hd, b, 0:v_dim, :] = v_bf[hd * v_dim:(hd + 1) * v_dim, b * vb:(b + 1) * vb]
                v_out[hd, b, v_dim:, :] = ones_row[:, b * vb:(b + 1) * vb]

    o = 0
    fq_t = proj_t(o, o + FOX_WIDTH); o += FOX_WIDTH
    fk_t = proj_t(o, o + FOX_WIDTH); o += FOX_WIDTH
    emit_qk(fq_t, fk_t, c_fox, fq_ref, fk_ref)
    fv_t = proj_t(o, o + FOX_WIDTH); o += FOX_WIDTH
    emit_v(fv_t, FOX_HEADS, HEAD_DIM, fv_ref)
    dq_t = proj_t(o, o + DIFF_WIDTH); o += DIFF_WIDTH
    dk_t = proj_t(o, o + DIFF_WIDTH); o += DIFF_WIDTH
    emit_qk(dq_t, dk_t, c_alibi, dq_ref, dk_ref)
    dv_t = proj_t(o, o + DIFF_WIDTH); o += DIFF_WIDTH
    emit_v(dv_t, DIFF_HEADS, DIFF_V_DIM, dv_ref)


def _projection(x2, g_pre_mix, w_in, b_forget):
    seq = x2.shape[0]
    rows = PROJ_ROWS
    n_t = seq // rows
    nvb = seq // ATT_BK
    vb_per_t = rows // ATT_BK
    f0 = 3 * FOX_WIDTH
    w_main = jnp.concatenate([w_in[:, :f0], w_in[:, f0 + FOX_HEADS:]], axis=1)
    w_f = jnp.pad(w_in[:, f0:f0 + FOX_HEADS], ((0, 0), (0, 16 - FOX_HEADS)))
    w_t = jnp.concatenate([w_main, w_f], axis=1).T.astype(_BF16)
    n_feat = w_t.shape[0]
    slopes = 2.0 ** (-8.0 * np.arange(1, DIFF_HEADS + 1) / DIFF_HEADS)
    slope_col = jnp.asarray(np.repeat(slopes, 2).reshape(DIFF_MAPS, 1), _F32)
    gq, gk = _bias_selectors()

    const2 = lambda t: (0, 0)
    out_shapes = (
        jax.ShapeDtypeStruct((FOX_HEADS, QK_FEATS, seq), _BF16),
        jax.ShapeDtypeStruct((FOX_HEADS, seq, QK_FEATS), _BF16),
        jax.ShapeDtypeStruct((FOX_HEADS, nvb, FOX_V_ROWS, ATT_BK), _BF16),
        jax.ShapeDtypeStruct((DIFF_MAPS, QK_FEATS, seq), _BF16),
        jax.ShapeDtypeStruct((DIFF_MAPS, seq, QK_FEATS), _BF16),
        jax.ShapeDtypeStruct((DIFF_HEADS, nvb, DIFF_V_ROWS, ATT_BK), _BF16),
    )
    out_specs = (
        pl.BlockSpec((FOX_HEADS, QK_FEATS, rows), lambda t: (0, 0, t)),
        pl.BlockSpec((FOX_HEADS, rows, QK_FEATS), lambda t: (0, t, 0)),
        pl.BlockSpec((FOX_HEADS, vb_per_t, FOX_V_ROWS, ATT_BK), lambda t: (0, t, 0, 0)),
        pl.BlockSpec((DIFF_MAPS, QK_FEATS, rows), lambda t: (0, 0, t)),
        pl.BlockSpec((DIFF_MAPS, rows, QK_FEATS), lambda t: (0, t, 0)),
        pl.BlockSpec((DIFF_HEADS, vb_per_t, DIFF_V_ROWS, ATT_BK), lambda t: (0, t, 0, 0)),
    )
    return pl.pallas_call(
        _proj_kernel,
        grid=(n_t,),
        in_specs=[
            pl.BlockSpec((rows, D_MODEL), lambda t: (t, 0)),
            pl.BlockSpec((1, D_MODEL), const2),
            pl.BlockSpec((n_feat, D_MODEL), const2),
            pl.BlockSpec((FOX_HEADS, 1), const2),
            pl.BlockSpec((DIFF_MAPS, 1), const2),
            pl.BlockSpec(gq.shape, const2),
            pl.BlockSpec(gk.shape, const2),
        ],
        out_specs=out_specs,
        out_shape=out_shapes,
        scratch_shapes=[pltpu.VMEM((FOX_HEADS, 128), _F32)],
        compiler_params=pltpu.CompilerParams(
            dimension_semantics=("arbitrary",), vmem_limit_bytes=VMEM_LIMIT_BYTES),
        name="proj",
    )(x2, g_pre_mix.reshape(1, D_MODEL), w_t, b_forget.reshape(FOX_HEADS, 1), slope_col, gq, gk)


def _softmax_t(s_t, m, mask):
    if mask is not None:
        s_t = jnp.where(mask, s_t, MASK_VALUE)
    bk, bq = s_t.shape
    parts = [jnp.max(s_t[r:r + MAX_CHUNK].reshape(MAX_CHUNK // 8, 8, bq), axis=0)
             for r in range(0, bk, MAX_CHUNK)]
    while len(parts) > 1:
        parts = [jnp.maximum(parts[i], parts[i + 1]) for i in range(0, len(parts), 2)]
    m_new = jnp.maximum(m, jnp.max(parts[0], axis=0, keepdims=True))
    alpha = jnp.exp2(m - m_new)
    p_t = jnp.exp2((s_t - m_new).astype(_BF16))
    return m_new, alpha, p_t


def _flash_pair(q_a, q_b, k_ref, k_a, k_b, v_ref, v_a, v_b, qi, v_rows, s_ref, p_ref, finish):
    bq, bk = ATT_BQ, ATT_BK
    n_diag = bq // bk
    n_full = qi * n_diag

    def qk(q_t, k_idx, j):
        start = pl.multiple_of(j * bk, bk)
        return jnp.dot(k_ref[k_idx, pl.ds(start, bk), :], q_t, preferred_element_type=_F32)

    def pv(v_idx, j, p_t):
        return jnp.dot(v_ref[v_idx, j], p_t, preferred_element_type=_F32)

    def diag_mask(d):
        row = lax.broadcasted_iota(jnp.int32, (bk, bq), 0)
        col = lax.broadcasted_iota(jnp.int32, (bk, bq), 1)
        return row + d * bk <= col

    def first(mask_a):
        m0 = jnp.full((1, bq), MASK_VALUE, _F32)
        m_a, _, p_a = _softmax_t(qk(q_a, k_a, 0), m0, mask_a)
        s_b = qk(q_b, k_b, 0)
        for slot in range(2):
            p_ref[slot] = p_a
            s_ref[slot] = s_b
        zero = jnp.zeros((v_rows, bq), _F32)
        return m_a, zero, m0, zero

    def step(i, carry, mask_a, mask_b, src):
        dst = 1 - src
        m_a, acc_a, m_b, acc_b = carry
        s_a = qk(q_a, k_a, i)
        acc_a = acc_a + pv(v_a, i - 1, p_ref[src])
        s_ref[dst] = qk(q_b, k_b, i)
        m_b, alpha_b, p_b = _softmax_t(s_ref[src], m_b, mask_b)
        acc_b = alpha_b * acc_b + pv(v_b, i - 1, p_b)
        m_a, alpha_a, p_a = _softmax_t(s_a, m_a, mask_a)
        acc_a = alpha_a * acc_a
        p_ref[dst] = p_a
        return m_a, acc_a, m_b, acc_b

    def last(carry, j_last, mask_b, src):
        m_a, acc_a, m_b, acc_b = carry
        acc_a = acc_a + pv(v_a, j_last, p_ref[src])
        m_b, alpha_b, p_b = _softmax_t(s_ref[src], m_b, mask_b)
        acc_b = alpha_b * acc_b + pv(v_b, j_last, p_b)
        finish(acc_a, acc_b)

    def diag_steps(carry, first_d, src):
        for d in range(first_d, n_diag):
            carry = step(n_full + d, carry, diag_mask(d), diag_mask(d - 1) if d else None, src)
            src = 1 - src
        last(carry, n_full + n_diag - 1, diag_mask(n_diag - 1), src)

    @pl.when(qi == 0)
    def _():
        diag_steps(first(diag_mask(0)), 1, 0)

    @pl.when(qi > 0)
    def _():
        carry = first(None)
        n_plain = n_full - 1
        odd = n_plain & 1
        carry = lax.cond(odd == 1, lambda c: step(1, c, None, None, 1), lambda c: c, carry)

        def pair(t, c):
            i = 1 + odd + 2 * t
            return step(i + 1, step(i, c, None, None, 0), None, None, 1)

        carry = lax.fori_loop(0, n_plain // 2, pair, carry)
        diag_steps(carry, 0, 0)


def _fox_kernel(q_ref, k_ref, v_ref, o_ref, s_ref, p_ref):
    def finish(acc_a, acc_b):
        outs = [acc[0:HEAD_DIM] / acc[HEAD_DIM:HEAD_DIM + 1] for acc in (acc_a, acc_b)]
        o_t = jnp.concatenate(outs, axis=0)
        o_ref[...] = o_t.T.astype(o_ref.dtype)

    _flash_pair(q_ref[0], q_ref[1], k_ref, 0, 1, v_ref, 0, 1, pl.program_id(1), FOX_V_ROWS,
                s_ref, p_ref, finish)


def _diff_kernel(q_ref, k_ref, v_ref, lam_ref, g_ref, o_ref, s_ref, p_ref):
    def finish(acc_a, acc_b):
        lp = lam_ref[...]
        lam = (jnp.exp(jnp.sum(lp[0:1] * lp[1:2], axis=1, keepdims=True))
               - jnp.exp(jnp.sum(lp[2:3] * lp[3:4], axis=1, keepdims=True)) + LAM_INIT)
        outs = [acc[0:DIFF_V_DIM] / acc[DIFF_V_DIM:DIFF_V_DIM + 1] for acc in (acc_a, acc_b)]
        o_t = outs[0] - lam * outs[1]
        o_t = ((o_t * _rms_scale(o_t, 0)) * g_ref[...]) * (1.0 - LAM_INIT)
        o_ref[...] = o_t.T.astype(o_ref.dtype)

    _flash_pair(q_ref[0], q_ref[1], k_ref, 0, 1, v_ref, 0, 0, pl.program_id(1), DIFF_V_ROWS,
                s_ref, p_ref, finish)


def _attention(q_t, k, v_t, kernel_fn, maps_per_step, v_per_step, extra=(), name=None):
    n_maps, _, seq = q_t.shape
    nvb, v_rows = v_t.shape[1], v_t.shape[2]
    n_groups = n_maps // maps_per_step
    extra_specs = [pl.BlockSpec(e.shape, lambda g, qi: (0, 0)) for e in extra]
    return pl.pallas_call(
        kernel_fn,
        grid=(n_groups, seq // ATT_BQ),
        in_specs=[
            pl.BlockSpec((maps_per_step, QK_FEATS, ATT_BQ), lambda g, qi: (g, 0, qi)),
            pl.BlockSpec((maps_per_step, seq, QK_FEATS), lambda g, qi: (g, 0, 0)),
            pl.BlockSpec((v_per_step, nvb, v_rows, ATT_BK), lambda g, qi: (g, 0, 0, 0)),
        ] + extra_specs,
        out_specs=pl.BlockSpec((ATT_BQ, 128), lambda g, qi: (qi, g)),
        out_shape=jax.ShapeDtypeStruct((seq, n_groups * 128), _BF16),
        scratch_shapes=[pltpu.VMEM((2, ATT_BK, ATT_BQ), _F32), pltpu.VMEM((2, ATT_BK, ATT_BQ), _BF16)],
        compiler_params=pltpu.CompilerParams(
            dimension_semantics=("arbitrary", "arbitrary"), vmem_limit_bytes=VMEM_LIMIT_BYTES),
        name=name,
    )(q_t, k, v_t, *extra)


def _out_kernel(x_ref, fox_ref, diff_ref, wo_ref, wu_ref, wd_ref, g_ref, o_ref):
    g = g_ref[...]
    mix = (jnp.dot(fox_ref[...], wo_ref[0:FOX_WIDTH, :], preferred_element_type=_F32)
           + jnp.dot(diff_ref[...], wo_ref[FOX_WIDTH:, :], preferred_element_type=_F32))
    x1 = x_ref[...] + (mix * _rms_scale(mix, -1)) * g[0:1]
    h = ((x1 * _rms_scale(x1, -1)) * g[1:2]).astype(_BF16)
    d = jnp.zeros_like(x1)
    for f in range(D_FF // FF_CHUNK):
        u = jnp.dot(h, wu_ref[:, f * FF_CHUNK:(f + 1) * FF_CHUNK], preferred_element_type=_F32)
        u = jnp.square(jnp.maximum(u, 0.0)).astype(_BF16)
        d = d + jnp.dot(u, wd_ref[f * FF_CHUNK:(f + 1) * FF_CHUNK, :], preferred_element_type=_F32)
    o_ref[...] = x1 + (d * _rms_scale(d, -1)) * g[2:3]


def _output_stage(x2, fox, diff, w_out, w_up, w_down, gains):
    seq = x2.shape[0]
    rows = OUT_ROWS
    const2 = lambda t: (0, 0)
    resident = functools.partial(pl.BlockSpec, index_map=const2, pipeline_mode=pl.Buffered(1))
    return pl.pallas_call(
        _out_kernel,
        grid=(seq // rows,),
        in_specs=[
            pl.BlockSpec((rows, D_MODEL), lambda t: (t, 0)),
            pl.BlockSpec((rows, FOX_WIDTH), lambda t: (t, 0)),
            pl.BlockSpec((rows, DIFF_WIDTH), lambda t: (t, 0)),
            resident((D_MODEL, D_MODEL)),
            resident((D_MODEL, D_FF)),
            resident((D_FF, D_MODEL)),
            pl.BlockSpec((3, D_MODEL), const2),
        ],
        out_specs=pl.BlockSpec((rows, D_MODEL), lambda t: (t, 0)),
        out_shape=jax.ShapeDtypeStruct((seq, D_MODEL), _F32),
        compiler_params=pltpu.CompilerParams(
            dimension_semantics=("arbitrary",), vmem_limit_bytes=VMEM_LIMIT_BYTES),
        name="out_mlp",
    )(x2, fox, diff, w_out, w_up, w_down, gains)


def kernel(x, g_pre_mix, w_in, b_forget, lambda_q1, lambda_k1, lambda_q2, lambda_k2,
           g_subln, w_out, g_post_mix, g_pre_mlp, w_up, w_down, g_post_mlp):
    batch, seq, _ = x.shape
    assert batch == 1 and w_in.shape[0] == 1
    assert seq % PROJ_ROWS == 0 and seq % ATT_BQ == 0 and seq % OUT_ROWS == 0
    assert ATT_BQ % ATT_BK == 0 and PROJ_ROWS % ATT_BK == 0
    x2 = x.reshape(seq, D_MODEL)
    fq_t, fk, fv_t, dq_t, dk, dv_t = _projection(x2, g_pre_mix[0], w_in[0], b_forget[0])
    fox = _attention(fq_t, fk, fv_t, _fox_kernel, 2, 2, name="fox_attn")
    lam_params = jnp.stack([lambda_q1[0], lambda_k1[0], lambda_q2[0], lambda_k2[0]]).astype(_F32)
    diff = _attention(dq_t, dk, dv_t, _diff_kernel, 2, 1,
                      extra=(lam_params, g_subln[0].reshape(DIFF_V_DIM, 1)), name="diff_attn")
    gains = jnp.stack([g_post_mix[0], g_pre_mlp[0], g_post_mlp[0]])
    out = _output_stage(x2, fox, diff, w_out[0].astype(_BF16), w_up[0].astype(_BF16),
                        w_down[0].astype(_BF16), gains)
    return out.reshape(batch, seq, D_MODEL)
```

```python
import functools
import math

import numpy as np
import jax
import jax.numpy as jnp
from jax import lax
from jax.experimental import pallas as pl
from jax.experimental.pallas import tpu as pltpu

D_MODEL = 1024
HEAD_DIM = 64
FOX_HEADS = 8
FOX_WIDTH = FOX_HEADS * HEAD_DIM
DIFF_HEADS = 4
DIFF_MAPS = 2 * DIFF_HEADS
DIFF_V_DIM = 2 * HEAD_DIM
DIFF_WIDTH = DIFF_HEADS * DIFF_V_DIM
D_FF = 4 * D_MODEL
RMS_EPS = 1e-6
LAM_INIT = 0.8 - 0.6 * math.exp(-0.3 * 0)
SCORE_SCALE = HEAD_DIM ** -0.5
LOG2E = math.log2(math.e)

QK_FEATS = 128
BIAS_ROWS = 16
FOX_V_ROWS = HEAD_DIM + BIAS_ROWS
DIFF_V_ROWS = DIFF_V_DIM + BIAS_ROWS
MASK_VALUE = -1e30

PROJ_ROWS = 512
ATT_BQ = 1024
ATT_BK = 1024
ATT_UNROLL = 2
ROW_CHUNK = 128
PV_TILE = 256
V_BLOCK = 512
OUT_ROWS = 512
FF_CHUNK = 1024
VMEM_LIMIT_BYTES = 56 * 1024 * 1024

_F32 = jnp.float32
_BF16 = jnp.bfloat16


def _split3(v):
    hi = v.astype(_BF16).astype(_F32)
    r = v - hi
    mid = r.astype(_BF16).astype(_F32)
    lo = (r - mid).astype(_BF16).astype(_F32)
    return hi, mid, lo


def _bias_selectors():
    gq = np.zeros((8 * BIAS_ROWS, 32), np.float32)
    gk = np.zeros((8 * BIAS_ROWS, 32), np.float32)
    for h in range(8):
        for part in range(3):
            gq[h * BIAS_ROWS + part, part * 8 + h] = 1.0
            gq[h * BIAS_ROWS + 3 + part, 24 + h] = 1.0
            gk[h * BIAS_ROWS + part, 24 + h] = 1.0
            gk[h * BIAS_ROWS + 3 + part, part * 8 + h] = -1.0
    return jnp.asarray(gq, _BF16), jnp.asarray(gk, _BF16)


def _rms_scale(v, axis):
    return lax.rsqrt(jnp.mean(v * v, axis=axis, keepdims=True) + RMS_EPS)


def _proj_kernel(x_ref, g_ref, w_ref, bf_ref, slope_ref, gq_ref, gk_ref,
                 fq_ref, fk_ref, fv_ref, dq_ref, dk_ref, dv_ref, carry_ref):
    t = pl.program_id(0)
    rows = x_ref.shape[0]

    @pl.when(t == 0)
    def _():
        carry_ref[...] = jnp.zeros_like(carry_ref)

    x = x_ref[...]
    h = ((x * _rms_scale(x, -1)) * g_ref[...]).astype(_BF16)

    def proj_t(lo, hi):
        return lax.dot_general(w_ref[lo:hi, :], h, (((1,), (1,)), ((), ())),
                               preferred_element_type=_F32)

    ones8 = jnp.ones((8, rows), _F32)

    def bias_feats(c):
        c_hi, c_mid, c_lo = _split3(c * LOG2E)
        c3 = jnp.concatenate([c_hi, c_mid, c_lo, ones8], axis=0).astype(_BF16)
        fq = jnp.dot(gq_ref[...], c3, preferred_element_type=_F32)
        fk = jnp.dot(gk_ref[...], c3, preferred_element_type=_F32)
        return fq, fk

    fl = proj_t(3 * FOX_WIDTH + 3 * DIFF_WIDTH, 3 * FOX_WIDTH + 3 * DIFF_WIDTH + 16)[0:8]
    fl = fl + bf_ref[...]
    ls = jnp.minimum(fl, 0.0) - jnp.log1p(jnp.exp(-jnp.abs(fl)))
    l_hi, l_mid, l_lo = _split3(ls)
    stack = jnp.concatenate([l_hi, l_mid, l_lo, jnp.zeros((8, rows), _F32)], axis=0).astype(_BF16)
    r_i = lax.broadcasted_iota(jnp.int32, (rows, rows), 0)
    c_i = lax.broadcasted_iota(jnp.int32, (rows, rows), 1)
    upper = (r_i <= c_i).astype(_BF16)
    cs = jnp.dot(stack, upper, preferred_element_type=_F32)
    c_fox = carry_ref[:, 0:1] + ((cs[0:8] + cs[8:16]) + cs[16:24])
    carry_ref[...] = jnp.broadcast_to(c_fox[:, rows - 1:rows], carry_ref.shape)

    pos = (lax.broadcasted_iota(jnp.int32, (8, rows), 1) + t * rows).astype(_F32)
    c_alibi = -(slope_ref[...] * pos)

    ones_row = (lax.broadcasted_iota(jnp.int32, (BIAS_ROWS, rows), 0) == 0).astype(_BF16)
    zero_pad_q = jnp.zeros((QK_FEATS - HEAD_DIM - BIAS_ROWS, rows), _BF16)
    zero_pad_k = jnp.zeros((QK_FEATS - HEAD_DIM - BIAS_ROWS, rows), _F32)
    vb = fv_ref.shape[-1]

    def emit_qk(q_t, k_t, c, q_out, k_out):
        feat_q, feat_k = bias_feats(c)
        for m in range(8):
            d0, b0 = m * HEAD_DIM, m * BIAS_ROWS
            q_out[m, 0:HEAD_DIM, :] = (q_t[d0:d0 + HEAD_DIM] * (SCORE_SCALE * LOG2E)).astype(_BF16)
            q_out[m, HEAD_DIM:HEAD_DIM + BIAS_ROWS, :] = feat_q[b0:b0 + BIAS_ROWS].astype(_BF16)
            q_out[m, HEAD_DIM + BIAS_ROWS:, :] = zero_pad_q
            k_aug = jnp.concatenate(
                [k_t[d0:d0 + HEAD_DIM], feat_k[b0:b0 + BIAS_ROWS], zero_pad_k], axis=0)
            k_out[m] = k_aug.T.astype(_BF16)

    def emit_v(v_t, n_heads, v_dim, v_out):
        v_bf = v_t.astype(_BF16)
        for hd in range(n_heads):
            for b in range(rows // vb):
                v_out[hd, b, 0:v_dim, :] = v_bf[hd * v_dim:(hd + 1) * v_dim, b * vb:(b + 1) * vb]
                v_out[hd, b, v_dim:, :] = ones_row[:, b * vb:(b + 1) * vb]

    o = 0
    fq_t = proj_t(o, o + FOX_WIDTH); o += FOX_WIDTH
    fk_t = proj_t(o, o + FOX_WIDTH); o += FOX_WIDTH
    emit_qk(fq_t, fk_t, c_fox, fq_ref, fk_ref)
    fv_t = proj_t(o, o + FOX_WIDTH); o += FOX_WIDTH
    emit_v(fv_t, FOX_HEADS, HEAD_DIM, fv_ref)
    dq_t = proj_t(o, o + DIFF_WIDTH); o += DIFF_WIDTH
    dk_t = proj_t(o, o + DIFF_WIDTH); o += DIFF_WIDTH
    emit_qk(dq_t, dk_t, c_alibi, dq_ref, dk_ref)
    dv_t = proj_t(o, o + DIFF_WIDTH); o += DIFF_WIDTH
    emit_v(dv_t, DIFF_HEADS, DIFF_V_DIM, dv_ref)


def _projection(x2, g_pre_mix, w_in, b_forget):
    seq = x2.shape[0]
    rows = PROJ_ROWS
    n_t = seq // rows
    nvb = seq // V_BLOCK
    vb_per_t = rows // V_BLOCK
    f0 = 3 * FOX_WIDTH
    w_main = jnp.concatenate([w_in[:, :f0], w_in[:, f0 + FOX_HEADS:]], axis=1)
    w_f = jnp.pad(w_in[:, f0:f0 + FOX_HEADS], ((0, 0), (0, 16 - FOX_HEADS)))
    w_t = jnp.concatenate([w_main, w_f], axis=1).T.astype(_BF16)
    n_feat = w_t.shape[0]
    slopes = 2.0 ** (-8.0 * np.arange(1, DIFF_HEADS + 1) / DIFF_HEADS)
    slope_col = jnp.asarray(np.repeat(slopes, 2).reshape(DIFF_MAPS, 1), _F32)
    gq, gk = _bias_selectors()

    const2 = lambda t: (0, 0)
    out_shapes = (
        jax.ShapeDtypeStruct((FOX_HEADS, QK_FEATS, seq), _BF16),
        jax.ShapeDtypeStruct((FOX_HEADS, seq, QK_FEATS), _BF16),
        jax.ShapeDtypeStruct((FOX_HEADS, nvb, FOX_V_ROWS, V_BLOCK), _BF16),
        jax.ShapeDtypeStruct((DIFF_MAPS, QK_FEATS, seq), _BF16),
        jax.ShapeDtypeStruct((DIFF_MAPS, seq, QK_FEATS), _BF16),
        jax.ShapeDtypeStruct((DIFF_HEADS, nvb, DIFF_V_ROWS, V_BLOCK), _BF16),
    )
    out_specs = (
        pl.BlockSpec((FOX_HEADS, QK_FEATS, rows), lambda t: (0, 0, t)),
        pl.BlockSpec((FOX_HEADS, rows, QK_FEATS), lambda t: (0, t, 0)),
        pl.BlockSpec((FOX_HEADS, vb_per_t, FOX_V_ROWS, V_BLOCK), lambda t: (0, t, 0, 0)),
        pl.BlockSpec((DIFF_MAPS, QK_FEATS, rows), lambda t: (0, 0, t)),
        pl.BlockSpec((DIFF_MAPS, rows, QK_FEATS), lambda t: (0, t, 0)),
        pl.BlockSpec((DIFF_HEADS, vb_per_t, DIFF_V_ROWS, V_BLOCK), lambda t: (0, t, 0, 0)),
    )
    return pl.pallas_call(
        _proj_kernel,
        grid=(n_t,),
        in_specs=[
            pl.BlockSpec((rows, D_MODEL), lambda t: (t, 0)),
            pl.BlockSpec((1, D_MODEL), const2),
            pl.BlockSpec((n_feat, D_MODEL), const2),
            pl.BlockSpec((FOX_HEADS, 1), const2),
            pl.BlockSpec((DIFF_MAPS, 1), const2),
            pl.BlockSpec(gq.shape, const2),
            pl.BlockSpec(gk.shape, const2),
        ],
        out_specs=out_specs,
        out_shape=out_shapes,
        scratch_shapes=[pltpu.VMEM((FOX_HEADS, 128), _F32)],
        compiler_params=pltpu.CompilerParams(
            dimension_semantics=("arbitrary",), vmem_limit_bytes=VMEM_LIMIT_BYTES),
        name="proj",
    )(x2, g_pre_mix.reshape(1, D_MODEL), w_t, b_forget.reshape(FOX_HEADS, 1), slope_col, gq, gk)


def _flash_chain(q_t, k_ref, k_idx, v_ref, v_idx, qi, v_rows, s_ref):
    bq, bk = ATT_BQ, ATT_BK
    n_diag = bq // bk
    n_full = qi * n_diag

    n_chunk = bk // ROW_CHUNK
    n_tile = bq // PV_TILE

    def chunk_max(s_c):
        return jnp.max(s_c.reshape(ROW_CHUNK // 8, 8, bq), axis=0)

    def park_chunk(j, c, slots):
        start = pl.multiple_of(j * bk + c * ROW_CHUNK, ROW_CHUNK)
        s_c = jnp.dot(k_ref[k_idx, pl.ds(start, ROW_CHUNK), :], q_t, preferred_element_type=_F32)
        for slot in slots:
            s_ref[slot, c * ROW_CHUNK:(c + 1) * ROW_CHUNK, :] = s_c
        return chunk_max(s_c)

    def park_scores(j, slots):
        return functools.reduce(jnp.maximum, [park_chunk(j, c, slots) for c in range(n_chunk)])

    def scores_chunk(src, c, mask):
        s_c = s_ref[src, c * ROW_CHUNK:(c + 1) * ROW_CHUNK, :]
        if mask is not None:
            row = lax.broadcasted_iota(jnp.int32, (ROW_CHUNK, bq), 0)
            col = lax.broadcasted_iota(jnp.int32, (ROW_CHUNK, bq), 1)
            s_c = jnp.where(row + (mask * bk + c * ROW_CHUNK) <= col, s_c, MASK_VALUE)
        return s_c

    chunks_per_tile = PV_TILE // ROW_CHUNK

    def step(i, carry, src, mask=None, has_next=True):
        m, acc, bmax = carry
        dst = 1 - src
        if mask is not None:
            bmax = functools.reduce(
                jnp.maximum, [chunk_max(scores_chunk(src, c, mask)) for c in range(n_chunk)])
        m_new = jnp.maximum(m, jnp.max(bmax, axis=0, keepdims=True))
        alpha = jnp.exp2(m - m_new)
        pv_sum = [None] * n_tile
        next_max = None
        p_parts = []
        for c in range(n_chunk):
            p_parts.append(jnp.exp2((scores_chunk(src, c, mask) - m_new).astype(_BF16)))
            if has_next:
                cm = park_chunk(i + 1, c, (dst,))
                next_max = cm if next_max is None else jnp.maximum(next_max, cm)
            if len(p_parts) == chunks_per_tile:
                kt = c // chunks_per_tile
                p_tile = jnp.concatenate(p_parts, axis=0)
                p_parts = []
                k0 = kt * PV_TILE
                v_tile = v_ref[v_idx, i * (bk // V_BLOCK) + k0 // V_BLOCK, :,
                               k0 % V_BLOCK:k0 % V_BLOCK + PV_TILE]
                for nt in range(n_tile):
                    piece = jnp.dot(v_tile, p_tile[:, nt * PV_TILE:(nt + 1) * PV_TILE],
                                    preferred_element_type=_F32)
                    pv_sum[nt] = piece if pv_sum[nt] is None else pv_sum[nt] + piece
        acc = alpha * acc + jnp.concatenate(pv_sum, axis=1)
        return m_new, acc, (next_max if has_next else bmax)

    bmax0 = park_scores(0, (0, 1))
    carry = (jnp.full((1, bq), MASK_VALUE, _F32), jnp.zeros((v_rows, bq), _F32), bmax0)

    done = 0
    width = 1
    while width < ATT_UNROLL:
        def leftover(c, done=done, width=width):
            src = width % 2
            for u in range(width):
                c = step(done + u, c, src)
                src = 1 - src
            return c
        take = (n_full & width) != 0
        carry = lax.cond(take, leftover, lambda c: c, carry)
        done = done + jnp.where(take, width, 0)
        width *= 2

    def trip(t, c):
        i = done + ATT_UNROLL * t
        src = 0
        for u in range(ATT_UNROLL):
            c = step(i + u, c, src)
            src = 1 - src
        return c

    carry = lax.fori_loop(0, n_full // ATT_UNROLL, trip, carry)

    src = 0
    for d in range(n_diag):
        carry = step(n_full + d, carry, src, mask=d, has_next=d + 1 < n_diag)
        src = 1 - src
    return carry[1]


def _fox_kernel(q_ref, k_ref, v_ref, o_ref, s_ref):
    qi = pl.program_id(1)
    outs = []
    for hh in range(2):
        acc = _flash_chain(q_ref[hh], k_ref, hh, v_ref, hh, qi, FOX_V_ROWS, s_ref)
        outs.append(acc[0:HEAD_DIM] / acc[HEAD_DIM:HEAD_DIM + 1])
    o_t = jnp.concatenate(outs, axis=0)
    o_ref[...] = o_t.T.astype(o_ref.dtype)


def _diff_kernel(q_ref, k_ref, v_ref, lam_ref, g_ref, o_ref, s_ref):
    qi = pl.program_id(1)
    outs = []
    for n in range(2):
        acc = _flash_chain(q_ref[n], k_ref, n, v_ref, 0, qi, DIFF_V_ROWS, s_ref)
        outs.append(acc[0:DIFF_V_DIM] / acc[DIFF_V_DIM:DIFF_V_DIM + 1])
    lp = lam_ref[...]
    lam = (jnp.exp(jnp.sum(lp[0:1] * lp[1:2], axis=1, keepdims=True))
           - jnp.exp(jnp.sum(lp[2:3] * lp[3:4], axis=1, keepdims=True)) + LAM_INIT)
    o_t = outs[0] - lam * outs[1]
    o_t = ((o_t * _rms_scale(o_t, 0)) * g_ref[...]) * (1.0 - LAM_INIT)
    o_ref[...] = o_t.T.astype(o_ref.dtype)


def _attention(q_t, k, v_t, kernel_fn, maps_per_step, v_per_step, extra=(), name=None):
    n_maps, _, seq = q_t.shape
    nvb, v_rows = v_t.shape[1], v_t.shape[2]
    n_groups = n_maps // maps_per_step
    extra_specs = [pl.BlockSpec(e.shape, lambda g, qi: (0, 0)) for e in extra]
    return pl.pallas_call(
        kernel_fn,
        grid=(n_groups, seq // ATT_BQ),
        in_specs=[
            pl.BlockSpec((maps_per_step, QK_FEATS, ATT_BQ), lambda g, qi: (g, 0, qi)),
            pl.BlockSpec((maps_per_step, seq, QK_FEATS), lambda g, qi: (g, 0, 0)),
            pl.BlockSpec((v_per_step, nvb, v_rows, V_BLOCK), lambda g, qi: (g, 0, 0, 0)),
        ] + extra_specs,
        out_specs=pl.BlockSpec((ATT_BQ, 128), lambda g, qi: (qi, g)),
        out_shape=jax.ShapeDtypeStruct((seq, n_groups * 128), _BF16),
        scratch_shapes=[pltpu.VMEM((2, ATT_BK, ATT_BQ), _F32)],
        compiler_params=pltpu.CompilerParams(
            dimension_semantics=("arbitrary", "arbitrary"), vmem_limit_bytes=VMEM_LIMIT_BYTES),
        name=name,
    )(q_t, k, v_t, *extra)


def _out_kernel(x_ref, fox_ref, diff_ref, wo_ref, wu_ref, wd_ref, g_ref, o_ref):
    g = g_ref[...]
    mix = (jnp.dot(fox_ref[...], wo_ref[0:FOX_WIDTH, :], preferred_element_type=_F32)
           + jnp.dot(diff_ref[...], wo_ref[FOX_WIDTH:, :], preferred_element_type=_F32))
    x1 = x_ref[...] + (mix * _rms_scale(mix, -1)) * g[0:1]
    h = ((x1 * _rms_scale(x1, -1)) * g[1:2]).astype(_BF16)
    d = jnp.zeros_like(x1)
    for f in range(D_FF // FF_CHUNK):
        u = jnp.dot(h, wu_ref[:, f * FF_CHUNK:(f + 1) * FF_CHUNK], preferred_element_type=_F32)
        u = jnp.square(jnp.maximum(u, 0.0)).astype(_BF16)
        d = d + jnp.dot(u, wd_ref[f * FF_CHUNK:(f + 1) * FF_CHUNK, :], preferred_element_type=_F32)
    o_ref[...] = x1 + (d * _rms_scale(d, -1)) * g[2:3]


def _output_stage(x2, fox, diff, w_out, w_up, w_down, gains):
    seq = x2.shape[0]
    rows = OUT_ROWS
    const2 = lambda t: (0, 0)
    resident = functools.partial(pl.BlockSpec, index_map=const2, pipeline_mode=pl.Buffered(1))
    return pl.pallas_call(
        _out_kernel,
        grid=(seq // rows,),
        in_specs=[
            pl.BlockSpec((rows, D_MODEL), lambda t: (t, 0)),
            pl.BlockSpec((rows, FOX_WIDTH), lambda t: (t, 0)),
            pl.BlockSpec((rows, DIFF_WIDTH), lambda t: (t, 0)),
            resident((D_MODEL, D_MODEL)),
            resident((D_MODEL, D_FF)),
            resident((D_FF, D_MODEL)),
            pl.BlockSpec((3, D_MODEL), const2),
        ],
        out_specs=pl.BlockSpec((rows, D_MODEL), lambda t: (t, 0)),
        out_shape=jax.ShapeDtypeStruct((seq, D_MODEL), _F32),
        compiler_params=pltpu.CompilerParams(
            dimension_semantics=("arbitrary",), vmem_limit_bytes=VMEM_LIMIT_BYTES),
        name="out_mlp",
    )(x2, fox, diff, w_out, w_up, w_down, gains)


def kernel(x, g_pre_mix, w_in, b_forget, lambda_q1, lambda_k1, lambda_q2, lambda_k2,
           g_subln, w_out, g_post_mix, g_pre_mlp, w_up, w_down, g_post_mlp):
    batch, seq, _ = x.shape
    assert batch == 1 and w_in.shape[0] == 1
    assert seq % PROJ_ROWS == 0 and seq % ATT_BQ == 0 and seq % OUT_ROWS == 0
    assert ATT_BQ % ATT_BK == 0 and PROJ_ROWS % V_BLOCK == 0 and ATT_BK % V_BLOCK == 0
    assert V_BLOCK % PV_TILE == 0
    x2 = x.reshape(seq, D_MODEL)
    fq_t, fk, fv_t, dq_t, dk, dv_t = _projection(x2, g_pre_mix[0], w_in[0], b_forget[0])
    fox = _attention(fq_t, fk, fv_t, _fox_kernel, 2, 2, name="fox_attn")
    lam_params = jnp.stack([lambda_q1[0], lambda_k1[0], lambda_q2[0], lambda_k2[0]]).astype(_F32)
    diff = _attention(dq_t, dk, dv_t, _diff_kernel, 2, 1,
                      extra=(lam_params, g_subln[0].reshape(DIFF_V_DIM, 1)), name="diff_attn")
    gains = jnp.stack([g_post_mix[0], g_pre_mlp[0], g_post_mlp[0]])
    out = _output_stage(x2, fox, diff, w_out[0].astype(_BF16), w_up[0].astype(_BF16),
                        w_down[0].astype(_BF16), gains)
    return out.reshape(batch, seq, D_MODEL)
```

```python
import functools
import math

import numpy as np
import jax
import jax.numpy as jnp
from jax import lax
from jax.experimental import pallas as pl
from jax.experimental.pallas import tpu as pltpu

D_MODEL = 1024
HEAD_DIM = 64
FOX_HEADS = 8
FOX_WIDTH = FOX_HEADS * HEAD_DIM
DIFF_HEADS = 4
DIFF_MAPS = 2 * DIFF_HEADS
DIFF_V_DIM = 2 * HEAD_DIM
DIFF_WIDTH = DIFF_HEADS * DIFF_V_DIM
D_FF = 4 * D_MODEL
RMS_EPS = 1e-6
LAM_INIT = 0.8 - 0.6 * math.exp(-0.3 * 0)
SCORE_SCALE = HEAD_DIM ** -0.5
LOG2E = math.log2(math.e)

QK_FEATS = 128
BIAS_ROWS = 16
FOX_V_ROWS = HEAD_DIM + BIAS_ROWS
DIFF_V_ROWS = DIFF_V_DIM + BIAS_ROWS
MASK_VALUE = -1e30

PROJ_ROWS = 512
ATT_BLOCK = 1024
ROW_CHUNK = 256
PV_TILE = 256
V_BLOCK = 512
OUT_ROWS = 512
FF_CHUNK = 1024
VMEM_LIMIT_BYTES = 56 * 1024 * 1024

_F32 = jnp.float32
_BF16 = jnp.bfloat16


def _split3(v):
    hi = v.astype(_BF16).astype(_F32)
    r = v - hi
    mid = r.astype(_BF16).astype(_F32)
    lo = (r - mid).astype(_BF16).astype(_F32)
    return hi, mid, lo


def _bias_selectors():
    gq = np.zeros((8 * BIAS_ROWS, 32), np.float32)
    gk = np.zeros((8 * BIAS_ROWS, 32), np.float32)
    for h in range(8):
        for part in range(3):
            gq[h * BIAS_ROWS + part, part * 8 + h] = 1.0
            gq[h * BIAS_ROWS + 3 + part, 24 + h] = 1.0
            gk[h * BIAS_ROWS + part, 24 + h] = 1.0
            gk[h * BIAS_ROWS + 3 + part, part * 8 + h] = -1.0
    return jnp.asarray(gq, _BF16), jnp.asarray(gk, _BF16)


def _rms_scale(v, axis):
    return lax.rsqrt(jnp.mean(v * v, axis=axis, keepdims=True) + RMS_EPS)


def _proj_kernel(x_ref, g_ref, w_ref, bf_ref, slope_ref, gq_ref, gk_ref,
                 fq_ref, fk_ref, fv_ref, dq_ref, dk_ref, dv_ref, carry_ref):
    t = pl.program_id(0)
    rows = x_ref.shape[0]

    @pl.when(t == 0)
    def _():
        carry_ref[...] = jnp.zeros_like(carry_ref)

    x = x_ref[...]
    h = ((x * _rms_scale(x, -1)) * g_ref[...]).astype(_BF16)

    def proj_t(lo, hi):
        return lax.dot_general(w_ref[lo:hi, :], h, (((1,), (1,)), ((), ())),
                               preferred_element_type=_F32)

    ones8 = jnp.ones((8, rows), _F32)

    def bias_feats(c):
        c_hi, c_mid, c_lo = _split3(c * LOG2E)
        c3 = jnp.concatenate([c_hi, c_mid, c_lo, ones8], axis=0).astype(_BF16)
        fq = jnp.dot(gq_ref[...], c3, preferred_element_type=_F32)
        fk = jnp.dot(gk_ref[...], c3, preferred_element_type=_F32)
        return fq, fk

    fl = proj_t(3 * FOX_WIDTH + 3 * DIFF_WIDTH, 3 * FOX_WIDTH + 3 * DIFF_WIDTH + 16)[0:8]
    fl = fl + bf_ref[...]
    ls = jnp.minimum(fl, 0.0) - jnp.log1p(jnp.exp(-jnp.abs(fl)))
    l_hi, l_mid, l_lo = _split3(ls)
    stack = jnp.concatenate([l_hi, l_mid, l_lo, jnp.zeros((8, rows), _F32)], axis=0).astype(_BF16)
    r_i = lax.broadcasted_iota(jnp.int32, (rows, rows), 0)
    c_i = lax.broadcasted_iota(jnp.int32, (rows, rows), 1)
    upper = (r_i <= c_i).astype(_BF16)
    cs = jnp.dot(stack, upper, preferred_element_type=_F32)
    c_fox = carry_ref[:, 0:1] + ((cs[0:8] + cs[8:16]) + cs[16:24])
    carry_ref[...] = jnp.broadcast_to(c_fox[:, rows - 1:rows], carry_ref.shape)

    pos = (lax.broadcasted_iota(jnp.int32, (8, rows), 1) + t * rows).astype(_F32)
    c_alibi = -(slope_ref[...] * pos)

    ones_row = (lax.broadcasted_iota(jnp.int32, (BIAS_ROWS, rows), 0) == 0).astype(_BF16)
    zero_pad_q = jnp.zeros((QK_FEATS - HEAD_DIM - BIAS_ROWS, rows), _BF16)
    zero_pad_k = jnp.zeros((QK_FEATS - HEAD_DIM - BIAS_ROWS, rows), _F32)
    vb = fv_ref.shape[-1]

    def emit_qk(q_t, k_t, c, q_out, k_out):
        feat_q, feat_k = bias_feats(c)
        for m in range(8):
            d0, b0 = m * HEAD_DIM, m * BIAS_ROWS
            q_out[m, 0:HEAD_DIM, :] = (q_t[d0:d0 + HEAD_DIM] * (SCORE_SCALE * LOG2E)).astype(_BF16)
            q_out[m, HEAD_DIM:HEAD_DIM + BIAS_ROWS, :] = feat_q[b0:b0 + BIAS_ROWS].astype(_BF16)
            q_out[m, HEAD_DIM + BIAS_ROWS:, :] = zero_pad_q
            k_aug = jnp.concatenate(
                [k_t[d0:d0 + HEAD_DIM], feat_k[b0:b0 + BIAS_ROWS], zero_pad_k], axis=0)
            k_out[m] = k_aug.T.astype(_BF16)

    def emit_v(v_t, n_heads, v_dim, v_out):
        v_bf = v_t.astype(_BF16)
        for hd in range(n_heads):
            for b in range(rows // vb):
                v_out[hd, b, 0:v_dim, :] = v_bf[hd * v_dim:(hd + 1) * v_dim, b * vb:(b + 1) * vb]
                v_out[hd, b, v_dim:, :] = ones_row[:, b * vb:(b + 1) * vb]

    o = 0
    fq_t = proj_t(o, o + FOX_WIDTH); o += FOX_WIDTH
    fk_t = proj_t(o, o + FOX_WIDTH); o += FOX_WIDTH
    emit_qk(fq_t, fk_t, c_fox, fq_ref, fk_ref)
    fv_t = proj_t(o, o + FOX_WIDTH); o += FOX_WIDTH
    emit_v(fv_t, FOX_HEADS, HEAD_DIM, fv_ref)
    dq_t = proj_t(o, o + DIFF_WIDTH); o += DIFF_WIDTH
    dk_t = proj_t(o, o + DIFF_WIDTH); o += DIFF_WIDTH
    emit_qk(dq_t, dk_t, c_alibi, dq_ref, dk_ref)
    dv_t = proj_t(o, o + DIFF_WIDTH); o += DIFF_WIDTH
    emit_v(dv_t, DIFF_HEADS, DIFF_V_DIM, dv_ref)


def _projection(x2, g_pre_mix, w_in, b_forget):
    seq = x2.shape[0]
    rows = PROJ_ROWS
    n_t = seq // rows
    nvb = seq // V_BLOCK
    vb_per_t = rows // V_BLOCK
    f0 = 3 * FOX_WIDTH
    w_main = jnp.concatenate([w_in[:, :f0], w_in[:, f0 + FOX_HEADS:]], axis=1)
    w_f = jnp.pad(w_in[:, f0:f0 + FOX_HEADS], ((0, 0), (0, 16 - FOX_HEADS)))
    w_t = jnp.concatenate([w_main, w_f], axis=1).T.astype(_BF16)
    n_feat = w_t.shape[0]
    slopes = 2.0 ** (-8.0 * np.arange(1, DIFF_HEADS + 1) / DIFF_HEADS)
    slope_col = jnp.asarray(np.repeat(slopes, 2).reshape(DIFF_MAPS, 1), _F32)
    gq, gk = _bias_selectors()

    const2 = lambda t: (0, 0)
    out_shapes = (
        jax.ShapeDtypeStruct((FOX_HEADS, QK_FEATS, seq), _BF16),
        jax.ShapeDtypeStruct((FOX_HEADS, seq, QK_FEATS), _BF16),
        jax.ShapeDtypeStruct((FOX_HEADS, nvb, FOX_V_ROWS, V_BLOCK), _BF16),
        jax.ShapeDtypeStruct((DIFF_MAPS, QK_FEATS, seq), _BF16),
        jax.ShapeDtypeStruct((DIFF_MAPS, seq, QK_FEATS), _BF16),
        jax.ShapeDtypeStruct((DIFF_HEADS, nvb, DIFF_V_ROWS, V_BLOCK), _BF16),
    )
    out_specs = (
        pl.BlockSpec((FOX_HEADS, QK_FEATS, rows), lambda t: (0, 0, t)),
        pl.BlockSpec((FOX_HEADS, rows, QK_FEATS), lambda t: (0, t, 0)),
        pl.BlockSpec((FOX_HEADS, vb_per_t, FOX_V_ROWS, V_BLOCK), lambda t: (0, t, 0, 0)),
        pl.BlockSpec((DIFF_MAPS, QK_FEATS, rows), lambda t: (0, 0, t)),
        pl.BlockSpec((DIFF_MAPS, rows, QK_FEATS), lambda t: (0, t, 0)),
        pl.BlockSpec((DIFF_HEADS, vb_per_t, DIFF_V_ROWS, V_BLOCK), lambda t: (0, t, 0, 0)),
    )
    return pl.pallas_call(
        _proj_kernel,
        grid=(n_t,),
        in_specs=[
            pl.BlockSpec((rows, D_MODEL), lambda t: (t, 0)),
            pl.BlockSpec((1, D_MODEL), const2),
            pl.BlockSpec((n_feat, D_MODEL), const2),
            pl.BlockSpec((FOX_HEADS, 1), const2),
            pl.BlockSpec((DIFF_MAPS, 1), const2),
            pl.BlockSpec(gq.shape, const2),
            pl.BlockSpec(gk.shape, const2),
        ],
        out_specs=out_specs,
        out_shape=out_shapes,
        scratch_shapes=[pltpu.VMEM((FOX_HEADS, 128), _F32)],
        compiler_params=pltpu.CompilerParams(
            dimension_semantics=("arbitrary",), vmem_limit_bytes=VMEM_LIMIT_BYTES),
        name="proj",
    )(x2, g_pre_mix.reshape(1, D_MODEL), w_t, b_forget.reshape(FOX_HEADS, 1), slope_col, gq, gk)


def _flash_chains(chains, k_ref, v_ref, qi, v_rows, s_ref):
    blk = ATT_BLOCK
    n_full = qi
    n_chunk = blk // ROW_CHUNK
    n_tile = blk // PV_TILE
    chunks_per_tile = PV_TILE // ROW_CHUNK

    def chunk_max(s_c):
        return jnp.max(s_c.reshape(ROW_CHUNK // 8, 8, s_c.shape[-1]), axis=0)

    def park_chunk(chain, j, c, slots):
        q_t, k_idx, _ = chain
        start = pl.multiple_of(j * blk + c * ROW_CHUNK, ROW_CHUNK)
        s_c = jnp.dot(k_ref[k_idx, pl.ds(start, ROW_CHUNK), :], q_t, preferred_element_type=_F32)
        for slot in slots:
            s_ref[slot, c * ROW_CHUNK:(c + 1) * ROW_CHUNK, :] = s_c
        return chunk_max(s_c)

    def diag_scores(src, c):
        lo = c * ROW_CHUNK
        s_c = s_ref[src, lo:lo + ROW_CHUNK, lo:]
        row = lax.broadcasted_iota(jnp.int32, s_c.shape, 0)
        col = lax.broadcasted_iota(jnp.int32, s_c.shape, 1)
        return jnp.where(row <= col, s_c, MASK_VALUE)

    def step(chain, i, carry, src, diagonal=False, park=None):
        m, acc, bmax = carry
        v_idx = chain[2]
        if diagonal:
            parts = [chunk_max(diag_scores(src, c)) for c in range(n_chunk)]
            bmax = jnp.concatenate(
                [functools.reduce(jnp.maximum,
                                  [parts[c][:, (b - c) * ROW_CHUNK:(b - c + 1) * ROW_CHUNK]
                                   for c in range(b + 1)])
                 for b in range(n_chunk)], axis=1)
        m_new = jnp.maximum(m, jnp.max(bmax, axis=0, keepdims=True))
        alpha = jnp.exp2(m - m_new)
        pv_sum = [None] * n_tile
        next_max = None
        p_parts = []
        for c in range(n_chunk):
            lo = c * ROW_CHUNK
            if diagonal:
                p_c = jnp.exp2((diag_scores(src, c) - m_new[:, lo:]).astype(_BF16))
                if lo:
                    p_c = jnp.concatenate([jnp.zeros((ROW_CHUNK, lo), _BF16), p_c], axis=1)
            else:
                p_c = jnp.exp2((s_ref[src, lo:lo + ROW_CHUNK, :] - m_new).astype(_BF16))
            p_parts.append(p_c)
            if park is not None:
                cm = park_chunk(park[0], park[1], c, park[2])
                next_max = cm if next_max is None else jnp.maximum(next_max, cm)
            if len(p_parts) == chunks_per_tile:
                kt = lo // PV_TILE
                p_tile = jnp.concatenate(p_parts, axis=0)
                p_parts = []
                k0 = kt * PV_TILE
                v_tile = v_ref[v_idx, i * (blk // V_BLOCK) + k0 // V_BLOCK, :,
                               k0 % V_BLOCK:k0 % V_BLOCK + PV_TILE]
                for nt in range(kt if diagonal else 0, n_tile):
                    piece = jnp.dot(v_tile, p_tile[:, nt * PV_TILE:(nt + 1) * PV_TILE],
                                    preferred_element_type=_F32)
                    pv_sum[nt] = piece if pv_sum[nt] is None else pv_sum[nt] + piece
        acc = alpha * acc + jnp.concatenate(pv_sum, axis=1)
        return m_new, acc, (bmax if park is None else next_max)

    bmax = functools.reduce(
        jnp.maximum, [park_chunk(chains[0], 0, c, (0, 1)) for c in range(n_chunk)])
    accs = []
    for n, chain in enumerate(chains):
        carry = (jnp.full((1, blk), MASK_VALUE, _F32), jnp.zeros((v_rows, blk), _F32), bmax)

        def trip(i, c, chain=chain):
            return lax.cond(
                ((i + n_full) & 1) == 0,
                lambda c: step(chain, i, c, 0, park=(chain, i + 1, (1,))),
                lambda c: step(chain, i, c, 1, park=(chain, i + 1, (0,))), c)

        carry = lax.fori_loop(0, n_full, trip, carry)
        park = (chains[n + 1], 0, (0, 1)) if n + 1 < len(chains) else None
        _, acc, bmax = step(chain, n_full, carry, 0, diagonal=True, park=park)
        accs.append(acc)
    return accs


def _fox_kernel(q_ref, k_ref, v_ref, o_ref, s_ref):
    accs = _flash_chains([(q_ref[hh], hh, hh) for hh in range(2)], k_ref, v_ref,
                         pl.program_id(1), FOX_V_ROWS, s_ref)
    outs = [acc[0:HEAD_DIM] / acc[HEAD_DIM:HEAD_DIM + 1] for acc in accs]
    o_t = jnp.concatenate(outs, axis=0)
    o_ref[...] = o_t.T.astype(o_ref.dtype)


def _diff_kernel(q_ref, k_ref, v_ref, lam_ref, g_ref, o_ref, s_ref):
    accs = _flash_chains([(q_ref[n], n, 0) for n in range(2)], k_ref, v_ref,
                         pl.program_id(1), DIFF_V_ROWS, s_ref)
    outs = [acc[0:DIFF_V_DIM] / acc[DIFF_V_DIM:DIFF_V_DIM + 1] for acc in accs]
    lp = lam_ref[...]
    lam = (jnp.exp(jnp.sum(lp[0:1] * lp[1:2], axis=1, keepdims=True))
           - jnp.exp(jnp.sum(lp[2:3] * lp[3:4], axis=1, keepdims=True)) + LAM_INIT)
    o_t = outs[0] - lam * outs[1]
    o_t = ((o_t * _rms_scale(o_t, 0)) * g_ref[...]) * (1.0 - LAM_INIT)
    o_ref[...] = o_t.T.astype(o_ref.dtype)


def _attention(q_t, k, v_t, kernel_fn, maps_per_step, v_per_step, extra=(), name=None):
    n_maps, _, seq = q_t.shape
    nvb, v_rows = v_t.shape[1], v_t.shape[2]
    n_groups = n_maps // maps_per_step
    extra_specs = [pl.BlockSpec(e.shape, lambda g, qi: (0, 0)) for e in extra]
    return pl.pallas_call(
        kernel_fn,
        grid=(n_groups, seq // ATT_BLOCK),
        in_specs=[
            pl.BlockSpec((maps_per_step, QK_FEATS, ATT_BLOCK), lambda g, qi: (g, 0, qi)),
            pl.BlockSpec((maps_per_step, seq, QK_FEATS), lambda g, qi: (g, 0, 0)),
            pl.BlockSpec((v_per_step, nvb, v_rows, V_BLOCK), lambda g, qi: (g, 0, 0, 0)),
        ] + extra_specs,
        out_specs=pl.BlockSpec((ATT_BLOCK, 128), lambda g, qi: (qi, g)),
        out_shape=jax.ShapeDtypeStruct((seq, n_groups * 128), _BF16),
        scratch_shapes=[pltpu.VMEM((2, ATT_BLOCK, ATT_BLOCK), _F32)],
        compiler_params=pltpu.CompilerParams(
            dimension_semantics=("arbitrary", "arbitrary"), vmem_limit_bytes=VMEM_LIMIT_BYTES),
        name=name,
    )(q_t, k, v_t, *extra)


def _out_kernel(x_ref, fox_ref, diff_ref, wo_ref, wu_ref, wd_ref, g_ref, o_ref):
    g = g_ref[...]
    mix = (jnp.dot(fox_ref[...], wo_ref[0:FOX_WIDTH, :], preferred_element_type=_F32)
           + jnp.dot(diff_ref[...], wo_ref[FOX_WIDTH:, :], preferred_element_type=_F32))
    x1 = x_ref[...] + (mix * _rms_scale(mix, -1)) * g[0:1]
    h = ((x1 * _rms_scale(x1, -1)) * g[1:2]).astype(_BF16)
    d = jnp.zeros_like(x1)
    for f in range(D_FF // FF_CHUNK):
        u = jnp.dot(h, wu_ref[:, f * FF_CHUNK:(f + 1) * FF_CHUNK], preferred_element_type=_F32)
        u = jnp.square(jnp.maximum(u, 0.0)).astype(_BF16)
        d = d + jnp.dot(u, wd_ref[f * FF_CHUNK:(f + 1) * FF_CHUNK, :], preferred_element_type=_F32)
    o_ref[...] = x1 + (d * _rms_scale(d, -1)) * g[2:3]


def _output_stage(x2, fox, diff, w_out, w_up, w_down, gains):
    seq = x2.shape[0]
    rows = OUT_ROWS
    const2 = lambda t: (0, 0)
    resident = functools.partial(pl.BlockSpec, index_map=const2, pipeline_mode=pl.Buffered(1))
    return pl.pallas_call(
        _out_kernel,
        grid=(seq // rows,),
        in_specs=[
            pl.BlockSpec((rows, D_MODEL), lambda t: (t, 0)),
            pl.BlockSpec((rows, FOX_WIDTH), lambda t: (t, 0)),
            pl.BlockSpec((rows, DIFF_WIDTH), lambda t: (t, 0)),
            resident((D_MODEL, D_MODEL)),
            resident((D_MODEL, D_FF)),
            resident((D_FF, D_MODEL)),
            pl.BlockSpec((3, D_MODEL), const2),
        ],
        out_specs=pl.BlockSpec((rows, D_MODEL), lambda t: (t, 0)),
        out_shape=jax.ShapeDtypeStruct((seq, D_MODEL), _F32),
        compiler_params=pltpu.CompilerParams(
            dimension_semantics=("arbitrary",), vmem_limit_bytes=VMEM_LIMIT_BYTES),
        name="out_mlp",
    )(x2, fox, diff, w_out, w_up, w_down, gains)


def kernel(x, g_pre_mix, w_in, b_forget, lambda_q1, lambda_k1, lambda_q2, lambda_k2,
           g_subln, w_out, g_post_mix, g_pre_mlp, w_up, w_down, g_post_mlp):
    batch, seq, _ = x.shape
    assert batch == 1 and w_in.shape[0] == 1
    assert seq % PROJ_ROWS == 0 and seq % ATT_BLOCK == 0 and seq % OUT_ROWS == 0
    assert PROJ_ROWS % V_BLOCK == 0 and ATT_BLOCK % V_BLOCK == 0
    assert V_BLOCK % PV_TILE == 0 and PV_TILE % ROW_CHUNK == 0
    x2 = x.reshape(seq, D_MODEL)
    fq_t, fk, fv_t, dq_t, dk, dv_t = _projection(x2, g_pre_mix[0], w_in[0], b_forget[0])
    fox = _attention(fq_t, fk, fv_t, _fox_kernel, 2, 2, name="fox_attn")
    lam_params = jnp.stack([lambda_q1[0], lambda_k1[0], lambda_q2[0], lambda_k2[0]]).astype(_F32)
    diff = _attention(dq_t, dk, dv_t, _diff_kernel, 2, 1,
                      extra=(lam_params, g_subln[0].reshape(DIFF_V_DIM, 1)), name="diff_attn")
    gains = jnp.stack([g_post_mix[0], g_pre_mlp[0], g_post_mlp[0]])
    out = _output_stage(x2, fox, diff, w_out[0].astype(_BF16), w_up[0].astype(_BF16),
                        w_down[0].astype(_BF16), gains)
    return out.reshape(batch, seq, D_MODEL)
```

```python
import functools
import math

import numpy as np
import jax
import jax.numpy as jnp
from jax import lax
from jax.experimental import pallas as pl
from jax.experimental.pallas import tpu as pltpu

D_MODEL = 1024
HEAD_DIM = 64
FOX_HEADS = 8
FOX_WIDTH = FOX_HEADS * HEAD_DIM
DIFF_HEADS = 4
DIFF_MAPS = 2 * DIFF_HEADS
DIFF_V_DIM = 2 * HEAD_DIM
DIFF_WIDTH = DIFF_HEADS * DIFF_V_DIM
D_FF = 4 * D_MODEL
RMS_EPS = 1e-6
LAM_INIT = 0.8 - 0.6 * math.exp(-0.3 * 0)
SCORE_SCALE = HEAD_DIM ** -0.5
LOG2E = math.log2(math.e)

QK_FEATS = 128
BIAS_ROWS = 16
FOX_V_ROWS = HEAD_DIM + BIAS_ROWS
DIFF_V_ROWS = DIFF_V_DIM + BIAS_ROWS
MASK_VALUE = -1e30

PROJ_ROWS = 512
ATT_BLOCK = 1024
ROW_CHUNK = 256
PV_TILE = 256
V_BLOCK = 512
OUT_ROWS = 512
FF_CHUNK = 1024
VMEM_LIMIT_BYTES = 56 * 1024 * 1024

_F32 = jnp.float32
_BF16 = jnp.bfloat16


def _split3(v):
    hi = v.astype(_BF16).astype(_F32)
    r = v - hi
    mid = r.astype(_BF16).astype(_F32)
    lo = (r - mid).astype(_BF16).astype(_F32)
    return hi, mid, lo


def _bias_selectors():
    gq = np.zeros((8 * BIAS_ROWS, 32), np.float32)
    gk = np.zeros((8 * BIAS_ROWS, 32), np.float32)
    for h in range(8):
        for part in range(3):
            gq[h * BIAS_ROWS + part, part * 8 + h] = 1.0
            gq[h * BIAS_ROWS + 3 + part, 24 + h] = 1.0
            gk[h * BIAS_ROWS + part, 24 + h] = 1.0
            gk[h * BIAS_ROWS + 3 + part, part * 8 + h] = -1.0
    return jnp.asarray(gq, _BF16), jnp.asarray(gk, _BF16)


def _rms_scale(v, axis):
    return lax.rsqrt(jnp.mean(v * v, axis=axis, keepdims=True) + RMS_EPS)


def _proj_kernel(x_ref, g_ref, w_ref, bf_ref, slope_ref, gq_ref, gk_ref,
                 fq_ref, fk_ref, fv_ref, dq_ref, dk_ref, dv_ref, carry_ref):
    t = pl.program_id(0)
    rows = x_ref.shape[0]

    @pl.when(t == 0)
    def _():
        carry_ref[...] = jnp.zeros_like(carry_ref)

    x = x_ref[...]
    h = ((x * _rms_scale(x, -1)) * g_ref[...]).astype(_BF16)

    def proj_t(lo, hi):
        return lax.dot_general(w_ref[lo:hi, :], h, (((1,), (1,)), ((), ())),
                               preferred_element_type=_F32)

    ones8 = jnp.ones((8, rows), _F32)

    def bias_feats(c):
        c_hi, c_mid, c_lo = _split3(c * LOG2E)
        c3 = jnp.concatenate([c_hi, c_mid, c_lo, ones8], axis=0).astype(_BF16)
        fq = jnp.dot(gq_ref[...], c3, preferred_element_type=_F32)
        fk = jnp.dot(gk_ref[...], c3, preferred_element_type=_F32)
        return fq, fk

    fl = proj_t(3 * FOX_WIDTH + 3 * DIFF_WIDTH, 3 * FOX_WIDTH + 3 * DIFF_WIDTH + 16)[0:8]
    fl = fl + bf_ref[...]
    ls = jnp.minimum(fl, 0.0) - jnp.log1p(jnp.exp(-jnp.abs(fl)))
    l_hi, l_mid, l_lo = _split3(ls)
    stack = jnp.concatenate([l_hi, l_mid, l_lo, jnp.zeros((8, rows), _F32)], axis=0).astype(_BF16)
    r_i = lax.broadcasted_iota(jnp.int32, (rows, rows), 0)
    c_i = lax.broadcasted_iota(jnp.int32, (rows, rows), 1)
    upper = (r_i <= c_i).astype(_BF16)
    cs = jnp.dot(stack, upper, preferred_element_type=_F32)
    c_fox = carry_ref[:, 0:1] + ((cs[0:8] + cs[8:16]) + cs[16:24])
    carry_ref[...] = jnp.broadcast_to(c_fox[:, rows - 1:rows], carry_ref.shape)

    pos = (lax.broadcasted_iota(jnp.int32, (8, rows), 1) + t * rows).astype(_F32)
    c_alibi = -(slope_ref[...] * pos)

    ones_row = (lax.broadcasted_iota(jnp.int32, (BIAS_ROWS, rows), 0) == 0).astype(_BF16)
    zero_pad_q = jnp.zeros((QK_FEATS - HEAD_DIM - BIAS_ROWS, rows), _BF16)
    zero_pad_k = jnp.zeros((QK_FEATS - HEAD_DIM - BIAS_ROWS, rows), _F32)
    vb = fv_ref.shape[-1]

    def emit_qk(q_t, k_t, c, q_out, k_out):
        feat_q, feat_k = bias_feats(c)
        for m in range(8):
            d0, b0 = m * HEAD_DIM, m * BIAS_ROWS
            q_out[m, 0:HEAD_DIM, :] = (q_t[d0:d0 + HEAD_DIM] * (SCORE_SCALE * LOG2E)).astype(_BF16)
            q_out[m, HEAD_DIM:HEAD_DIM + BIAS_ROWS, :] = feat_q[b0:b0 + BIAS_ROWS].astype(_BF16)
            q_out[m, HEAD_DIM + BIAS_ROWS:, :] = zero_pad_q
            k_aug = jnp.concatenate(
                [k_t[d0:d0 + HEAD_DIM], feat_k[b0:b0 + BIAS_ROWS], zero_pad_k], axis=0)
            k_out[m] = k_aug.T.astype(_BF16)

    def emit_v(v_t, n_heads, v_dim, v_out):
        v_bf = v_t.astype(_BF16)
        for hd in range(n_heads):
            for b in range(rows // vb):
                v_out[hd, b, 0:v_dim, :] = v_bf[hd * v_dim:(hd + 1) * v_dim, b * vb:(b + 1) * vb]
                v_out[hd, b, v_dim:, :] = ones_row[:, b * vb:(b + 1) * vb]

    o = 0
    fq_t = proj_t(o, o + FOX_WIDTH); o += FOX_WIDTH
    fk_t = proj_t(o, o + FOX_WIDTH); o += FOX_WIDTH
    emit_qk(fq_t, fk_t, c_fox, fq_ref, fk_ref)
    fv_t = proj_t(o, o + FOX_WIDTH); o += FOX_WIDTH
    emit_v(fv_t, FOX_HEADS, HEAD_DIM, fv_ref)
    dq_t = proj_t(o, o + DIFF_WIDTH); o += DIFF_WIDTH
    dk_t = proj_t(o, o + DIFF_WIDTH); o += DIFF_WIDTH
    emit_qk(dq_t, dk_t, c_alibi, dq_ref, dk_ref)
    dv_t = proj_t(o, o + DIFF_WIDTH); o += DIFF_WIDTH
    emit_v(dv_t, DIFF_HEADS, DIFF_V_DIM, dv_ref)


def _projection(x2, g_pre_mix, w_in, b_forget):
    seq = x2.shape[0]
    rows = PROJ_ROWS
    n_t = seq // rows
    nvb = seq // V_BLOCK
    vb_per_t = rows // V_BLOCK
    f0 = 3 * FOX_WIDTH
    w_main = jnp.concatenate([w_in[:, :f0], w_in[:, f0 + FOX_HEADS:]], axis=1)
    w_f = jnp.pad(w_in[:, f0:f0 + FOX_HEADS], ((0, 0), (0, 16 - FOX_HEADS)))
    w_t = jnp.concatenate([w_main, w_f], axis=1).T.astype(_BF16)
    n_feat = w_t.shape[0]
    slopes = 2.0 ** (-8.0 * np.arange(1, DIFF_HEADS + 1) / DIFF_HEADS)
    slope_col = jnp.asarray(np.repeat(slopes, 2).reshape(DIFF_MAPS, 1), _F32)
    gq, gk = _bias_selectors()

    const2 = lambda t: (0, 0)
    out_shapes = (
        jax.ShapeDtypeStruct((FOX_HEADS, QK_FEATS, seq), _BF16),
        jax.ShapeDtypeStruct((FOX_HEADS, seq, QK_FEATS), _BF16),
        jax.ShapeDtypeStruct((FOX_HEADS, nvb, FOX_V_ROWS, V_BLOCK), _BF16),
        jax.ShapeDtypeStruct((DIFF_MAPS, QK_FEATS, seq), _BF16),
        jax.ShapeDtypeStruct((DIFF_MAPS, seq, QK_FEATS), _BF16),
        jax.ShapeDtypeStruct((DIFF_HEADS, nvb, DIFF_V_ROWS, V_BLOCK), _BF16),
    )
    out_specs = (
        pl.BlockSpec((FOX_HEADS, QK_FEATS, rows), lambda t: (0, 0, t)),
        pl.BlockSpec((FOX_HEADS, rows, QK_FEATS), lambda t: (0, t, 0)),
        pl.BlockSpec((FOX_HEADS, vb_per_t, FOX_V_ROWS, V_BLOCK), lambda t: (0, t, 0, 0)),
        pl.BlockSpec((DIFF_MAPS, QK_FEATS, rows), lambda t: (0, 0, t)),
        pl.BlockSpec((DIFF_MAPS, rows, QK_FEATS), lambda t: (0, t, 0)),
        pl.BlockSpec((DIFF_HEADS, vb_per_t, DIFF_V_ROWS, V_BLOCK), lambda t: (0, t, 0, 0)),
    )
    return pl.pallas_call(
        _proj_kernel,
        grid=(n_t,),
        in_specs=[
            pl.BlockSpec((rows, D_MODEL), lambda t: (t, 0)),
            pl.BlockSpec((1, D_MODEL), const2),
            pl.BlockSpec((n_feat, D_MODEL), const2),
            pl.BlockSpec((FOX_HEADS, 1), const2),
            pl.BlockSpec((DIFF_MAPS, 1), const2),
            pl.BlockSpec(gq.shape, const2),
            pl.BlockSpec(gk.shape, const2),
        ],
        out_specs=out_specs,
        out_shape=out_shapes,
        scratch_shapes=[pltpu.VMEM((FOX_HEADS, 128), _F32)],
        compiler_params=pltpu.CompilerParams(
            dimension_semantics=("arbitrary",), vmem_limit_bytes=VMEM_LIMIT_BYTES),
        name="proj",
    )(x2, g_pre_mix.reshape(1, D_MODEL), w_t, b_forget.reshape(FOX_HEADS, 1), slope_col, gq, gk)


def _flash_chains(chains, k_ref, v_ref, qi, s0_ref, s1_ref, m_ref, acc_ref, bmax_ref):
    blk = ATT_BLOCK
    s_refs = (s0_ref, s1_ref)
    n_full = qi
    n_chunk = blk // ROW_CHUNK
    n_tile = blk // PV_TILE
    chunks_per_tile = PV_TILE // ROW_CHUNK

    def chunk_max(s_c):
        return jnp.max(s_c.reshape(ROW_CHUNK // 8, 8, s_c.shape[-1]), axis=0)

    def park_chunk(chain, j, c, slots):
        q_t, k_idx, _ = chain
        start = pl.multiple_of(j * blk + c * ROW_CHUNK, ROW_CHUNK)
        s_c = jnp.dot(k_ref[k_idx, pl.ds(start, ROW_CHUNK), :], q_t, preferred_element_type=_F32)
        for slot in slots:
            s_refs[slot][c * ROW_CHUNK:(c + 1) * ROW_CHUNK, :] = s_c
        return chunk_max(s_c)

    def diag_scores(src, c):
        lo = c * ROW_CHUNK
        s_c = s_refs[src][lo:lo + ROW_CHUNK, lo:]
        row = lax.broadcasted_iota(jnp.int32, s_c.shape, 0)
        col = lax.broadcasted_iota(jnp.int32, s_c.shape, 1)
        return jnp.where(row <= col, s_c, MASK_VALUE)

    def step(chain, i, src, diagonal=False, park=None):
        m = m_ref[...]
        bmax = bmax_ref[...]
        v_idx = chain[2]
        if diagonal:
            parts = [chunk_max(diag_scores(src, c)) for c in range(n_chunk)]
            bmax = jnp.concatenate(
                [functools.reduce(jnp.maximum,
                                  [parts[c][:, (b - c) * ROW_CHUNK:(b - c + 1) * ROW_CHUNK]
                                   for c in range(b + 1)])
                 for b in range(n_chunk)], axis=1)
        m_new = jnp.maximum(m, jnp.max(bmax, axis=0, keepdims=True))
        alpha = jnp.exp2(m - m_new)
        pv_sum = [None] * n_tile
        next_max = None
        p_parts = []
        for c in range(n_chunk):
            lo = c * ROW_CHUNK
            if diagonal:
                p_c = jnp.exp2((diag_scores(src, c) - m_new[:, lo:]).astype(_BF16))
                if lo:
                    p_c = jnp.concatenate([jnp.zeros((ROW_CHUNK, lo), _BF16), p_c], axis=1)
            else:
                p_c = jnp.exp2((s_refs[src][lo:lo + ROW_CHUNK, :] - m_new).astype(_BF16))
            p_parts.append(p_c)
            if park is not None:
                cm = park_chunk(park[0], park[1], c, park[2])
                next_max = cm if next_max is None else jnp.maximum(next_max, cm)
            if len(p_parts) == chunks_per_tile:
                kt = lo // PV_TILE
                p_tile = jnp.concatenate(p_parts, axis=0)
                p_parts = []
                k0 = kt * PV_TILE
                v_tile = v_ref[v_idx, i * (blk // V_BLOCK) + k0 // V_BLOCK, :,
                               k0 % V_BLOCK:k0 % V_BLOCK + PV_TILE]
                for nt in range(kt if diagonal else 0, n_tile):
                    piece = jnp.dot(v_tile, p_tile[:, nt * PV_TILE:(nt + 1) * PV_TILE],
                                    preferred_element_type=_F32)
                    pv_sum[nt] = piece if pv_sum[nt] is None else pv_sum[nt] + piece
        acc_ref[...] = alpha * acc_ref[...] + jnp.concatenate(pv_sum, axis=1)
        m_ref[...] = m_new
        if park is not None:
            bmax_ref[...] = next_max

    bmax_ref[...] = functools.reduce(
        jnp.maximum, [park_chunk(chains[0], 0, c, (0, 1)) for c in range(n_chunk)])
    accs = []
    for n, chain in enumerate(chains):
        m_ref[...] = jnp.full(m_ref.shape, MASK_VALUE, _F32)
        acc_ref[...] = jnp.zeros_like(acc_ref)

        def trip(i, _, chain=chain):
            lax.cond(((i + n_full) & 1) == 0,
                     lambda: step(chain, i, 0, park=(chain, i + 1, (1,))),
                     lambda: step(chain, i, 1, park=(chain, i + 1, (0,))))
            return 0

        lax.fori_loop(0, n_full, trip, 0)
        park = (chains[n + 1], 0, (0, 1)) if n + 1 < len(chains) else None
        step(chain, n_full, 0, diagonal=True, park=park)
        accs.append(acc_ref[...])
    return accs


def _fox_kernel(q_ref, k_ref, v_ref, o_ref, *scratch):
    accs = _flash_chains([(q_ref[hh], hh, hh) for hh in range(2)], k_ref, v_ref,
                         pl.program_id(1), *scratch)
    outs = [acc[0:HEAD_DIM] / acc[HEAD_DIM:HEAD_DIM + 1] for acc in accs]
    o_t = jnp.concatenate(outs, axis=0)
    o_ref[...] = o_t.T.astype(o_ref.dtype)


def _diff_kernel(q_ref, k_ref, v_ref, lam_ref, g_ref, o_ref, *scratch):
    accs = _flash_chains([(q_ref[n], n, 0) for n in range(2)], k_ref, v_ref,
                         pl.program_id(1), *scratch)
    outs = [acc[0:DIFF_V_DIM] / acc[DIFF_V_DIM:DIFF_V_DIM + 1] for acc in accs]
    lp = lam_ref[...]
    lam = (jnp.exp(jnp.sum(lp[0:1] * lp[1:2], axis=1, keepdims=True))
           - jnp.exp(jnp.sum(lp[2:3] * lp[3:4], axis=1, keepdims=True)) + LAM_INIT)
    o_t = outs[0] - lam * outs[1]
    o_t = ((o_t * _rms_scale(o_t, 0)) * g_ref[...]) * (1.0 - LAM_INIT)
    o_ref[...] = o_t.T.astype(o_ref.dtype)


def _attention(q_t, k, v_t, kernel_fn, maps_per_step, v_per_step, extra=(), name=None):
    n_maps, _, seq = q_t.shape
    nvb, v_rows = v_t.shape[1], v_t.shape[2]
    n_groups = n_maps // maps_per_step
    extra_specs = [pl.BlockSpec(e.shape, lambda g, qi: (0, 0)) for e in extra]
    return pl.pallas_call(
        kernel_fn,
        grid=(n_groups, seq // ATT_BLOCK),
        in_specs=[
            pl.BlockSpec((maps_per_step, QK_FEATS, ATT_BLOCK), lambda g, qi: (g, 0, qi)),
            pl.BlockSpec((maps_per_step, seq, QK_FEATS), lambda g, qi: (g, 0, 0)),
            pl.BlockSpec((v_per_step, nvb, v_rows, V_BLOCK), lambda g, qi: (g, 0, 0, 0)),
        ] + extra_specs,
        out_specs=pl.BlockSpec((ATT_BLOCK, 128), lambda g, qi: (qi, g)),
        out_shape=jax.ShapeDtypeStruct((seq, n_groups * 128), _BF16),
        scratch_shapes=[pltpu.VMEM((ATT_BLOCK, ATT_BLOCK), _F32),
                        pltpu.VMEM((ATT_BLOCK, ATT_BLOCK), _F32),
                        pltpu.VMEM((1, ATT_BLOCK), _F32),
                        pltpu.VMEM((v_rows, ATT_BLOCK), _F32),
                        pltpu.VMEM((8, ATT_BLOCK), _F32)],
        compiler_params=pltpu.CompilerParams(
            dimension_semantics=("arbitrary", "arbitrary"), vmem_limit_bytes=VMEM_LIMIT_BYTES),
        name=name,
    )(q_t, k, v_t, *extra)


def _out_kernel(x_ref, fox_ref, diff_ref, wo_ref, wu_ref, wd_ref, g_ref, o_ref):
    g = g_ref[...]
    mix = (jnp.dot(fox_ref[...], wo_ref[0:FOX_WIDTH, :], preferred_element_type=_F32)
           + jnp.dot(diff_ref[...], wo_ref[FOX_WIDTH:, :], preferred_element_type=_F32))
    x1 = x_ref[...] + (mix * _rms_scale(mix, -1)) * g[0:1]
    h = ((x1 * _rms_scale(x1, -1)) * g[1:2]).astype(_BF16)
    d = jnp.zeros_like(x1)
    for f in range(D_FF // FF_CHUNK):
        u = jnp.dot(h, wu_ref[:, f * FF_CHUNK:(f + 1) * FF_CHUNK], preferred_element_type=_F32)
        u = jnp.square(jnp.maximum(u, 0.0)).astype(_BF16)
        d = d + jnp.dot(u, wd_ref[f * FF_CHUNK:(f + 1) * FF_CHUNK, :], preferred_element_type=_F32)
    o_ref[...] = x1 + (d * _rms_scale(d, -1)) * g[2:3]


def _output_stage(x2, fox, diff, w_out, w_up, w_down, gains):
    seq = x2.shape[0]
    rows = OUT_ROWS
    const2 = lambda t: (0, 0)
    resident = functools.partial(pl.BlockSpec, index_map=const2, pipeline_mode=pl.Buffered(1))
    return pl.pallas_call(
        _out_kernel,
        grid=(seq // rows,),
        in_specs=[
            pl.BlockSpec((rows, D_MODEL), lambda t: (t, 0)),
            pl.BlockSpec((rows, FOX_WIDTH), lambda t: (t, 0)),
            pl.BlockSpec((rows, DIFF_WIDTH), lambda t: (t, 0)),
            resident((D_MODEL, D_MODEL)),
            resident((D_MODEL, D_FF)),
            resident((D_FF, D_MODEL)),
            pl.BlockSpec((3, D_MODEL), const2),
        ],
        out_specs=pl.BlockSpec((rows, D_MODEL), lambda t: (t, 0)),
        out_shape=jax.ShapeDtypeStruct((seq, D_MODEL), _F32),
        compiler_params=pltpu.CompilerParams(
            dimension_semantics=("arbitrary",), vmem_limit_bytes=VMEM_LIMIT_BYTES),
        name="out_mlp",
    )(x2, fox, diff, w_out, w_up, w_down, gains)


def kernel(x, g_pre_mix, w_in, b_forget, lambda_q1, lambda_k1, lambda_q2, lambda_k2,
           g_subln, w_out, g_post_mix, g_pre_mlp, w_up, w_down, g_post_mlp):
    batch, seq, _ = x.shape
    assert batch == 1 and w_in.shape[0] == 1
    assert seq % PROJ_ROWS == 0 and seq % ATT_BLOCK == 0 and seq % OUT_ROWS == 0
    assert PROJ_ROWS % V_BLOCK == 0 and ATT_BLOCK % V_BLOCK == 0
    assert V_BLOCK % PV_TILE == 0 and PV_TILE % ROW_CHUNK == 0
    x2 = x.reshape(seq, D_MODEL)
    fq_t, fk, fv_t, dq_t, dk, dv_t = _projection(x2, g_pre_mix[0], w_in[0], b_forget[0])
    fox = _attention(fq_t, fk, fv_t, _fox_kernel, 2, 2, name="fox_attn")
    lam_params = jnp.stack([lambda_q1[0], lambda_k1[0], lambda_q2[0], lambda_k2[0]]).astype(_F32)
    diff = _attention(dq_t, dk, dv_t, _diff_kernel, 2, 1,
                      extra=(lam_params, g_subln[0].reshape(DIFF_V_DIM, 1)), name="diff_attn")
    gains = jnp.stack([g_post_mix[0], g_pre_mlp[0], g_post_mlp[0]])
    out = _output_stage(x2, fox, diff, w_out[0].astype(_BF16), w_up[0].astype(_BF16),
                        w_down[0].astype(_BF16), gains)
    return out.reshape(batch, seq, D_MODEL)
```

```python
import functools
import math

import numpy as np
import jax
import jax.numpy as jnp
from jax import lax
from jax.experimental import pallas as pl
from jax.experimental.pallas import tpu as pltpu

D_MODEL = 1024
HEAD_DIM = 64
FOX_HEADS = 8
FOX_WIDTH = FOX_HEADS * HEAD_DIM
DIFF_HEADS = 4
DIFF_MAPS = 2 * DIFF_HEADS
DIFF_V_DIM = 2 * HEAD_DIM
DIFF_WIDTH = DIFF_HEADS * DIFF_V_DIM
D_FF = 4 * D_MODEL
RMS_EPS = 1e-6
LAM_INIT = 0.8 - 0.6 * math.exp(-0.3 * 0)
SCORE_SCALE = HEAD_DIM ** -0.5
LOG2E = math.log2(math.e)

QK_FEATS = 128
BIAS_ROWS = 16
FOX_V_ROWS = 128
DIFF_V_ROWS = DIFF_V_DIM + BIAS_ROWS
MASK_VALUE = -1e30

PROJ_ROWS = 512
ATT_BLOCK = 1024
ROW_CHUNK = 256
PV_TILE = 256
V_BLOCK = 512
OUT_ROWS = 512
FF_CHUNK = 1024
VMEM_LIMIT_BYTES = 56 * 1024 * 1024

_F32 = jnp.float32
_BF16 = jnp.bfloat16


def _split3(v):
    hi = v.astype(_BF16).astype(_F32)
    r = v - hi
    mid = r.astype(_BF16).astype(_F32)
    lo = (r - mid).astype(_BF16).astype(_F32)
    return hi, mid, lo


def _bias_selectors():
    gq = np.zeros((8 * BIAS_ROWS, 32), np.float32)
    gk = np.zeros((8 * BIAS_ROWS, 32), np.float32)
    for h in range(8):
        for part in range(3):
            gq[h * BIAS_ROWS + part, part * 8 + h] = 1.0
            gq[h * BIAS_ROWS + 3 + part, 24 + h] = 1.0
            gk[h * BIAS_ROWS + part, 24 + h] = 1.0
            gk[h * BIAS_ROWS + 3 + part, part * 8 + h] = -1.0
    return jnp.asarray(gq, _BF16), jnp.asarray(gk, _BF16)


def _rms_scale(v, axis):
    return lax.rsqrt(jnp.mean(v * v, axis=axis, keepdims=True) + RMS_EPS)


def _proj_kernel(x_ref, g_ref, w_ref, bf_ref, slope_ref, gq_ref, gk_ref,
                 fq_ref, fk_ref, fv_ref, dq_ref, dk_ref, dv_ref, carry_ref):
    t = pl.program_id(0)
    rows = x_ref.shape[0]

    @pl.when(t == 0)
    def _():
        carry_ref[...] = jnp.zeros_like(carry_ref)

    x = x_ref[...]
    h = ((x * _rms_scale(x, -1)) * g_ref[...]).astype(_BF16)

    def proj_t(lo, hi):
        return lax.dot_general(w_ref[lo:hi, :], h, (((1,), (1,)), ((), ())),
                               preferred_element_type=_F32)

    ones8 = jnp.ones((8, rows), _F32)

    def bias_feats(c):
        c_hi, c_mid, c_lo = _split3(c * LOG2E)
        c3 = jnp.concatenate([c_hi, c_mid, c_lo, ones8], axis=0).astype(_BF16)
        fq = jnp.dot(gq_ref[...], c3, preferred_element_type=_F32)
        fk = jnp.dot(gk_ref[...], c3, preferred_element_type=_F32)
        return fq, fk

    fl = proj_t(3 * FOX_WIDTH + 3 * DIFF_WIDTH, 3 * FOX_WIDTH + 3 * DIFF_WIDTH + 16)[0:8]
    fl = fl + bf_ref[...]
    ls = jnp.minimum(fl, 0.0) - jnp.log1p(jnp.exp(-jnp.abs(fl)))
    l_hi, l_mid, l_lo = _split3(ls)
    stack = jnp.concatenate([l_hi, l_mid, l_lo, jnp.zeros((8, rows), _F32)], axis=0).astype(_BF16)
    r_i = lax.broadcasted_iota(jnp.int32, (rows, rows), 0)
    c_i = lax.broadcasted_iota(jnp.int32, (rows, rows), 1)
    upper = (r_i <= c_i).astype(_BF16)
    cs = jnp.dot(stack, upper, preferred_element_type=_F32)
    c_fox = carry_ref[:, 0:1] + ((cs[0:8] + cs[8:16]) + cs[16:24])
    carry_ref[...] = jnp.broadcast_to(c_fox[:, rows - 1:rows], carry_ref.shape)

    pos = (lax.broadcasted_iota(jnp.int32, (8, rows), 1) + t * rows).astype(_F32)
    c_alibi = -(slope_ref[...] * pos)

    def ones_then_zeros(n):
        return (lax.broadcasted_iota(jnp.int32, (n, rows), 0) == 0).astype(_BF16)

    zero_pad_q = jnp.zeros((QK_FEATS - HEAD_DIM - BIAS_ROWS, rows), _BF16)
    zero_pad_k = jnp.zeros((QK_FEATS - HEAD_DIM - BIAS_ROWS, rows), _F32)
    vb = fv_ref.shape[-1]

    def emit_qk(q_t, k_t, c, q_out, k_out):
        feat_q, feat_k = bias_feats(c)
        for m in range(8):
            d0, b0 = m * HEAD_DIM, m * BIAS_ROWS
            q_out[m, 0:HEAD_DIM, :] = (q_t[d0:d0 + HEAD_DIM] * (SCORE_SCALE * LOG2E)).astype(_BF16)
            q_out[m, HEAD_DIM:HEAD_DIM + BIAS_ROWS, :] = feat_q[b0:b0 + BIAS_ROWS].astype(_BF16)
            q_out[m, HEAD_DIM + BIAS_ROWS:, :] = zero_pad_q
            k_aug = jnp.concatenate(
                [k_t[d0:d0 + HEAD_DIM], feat_k[b0:b0 + BIAS_ROWS], zero_pad_k], axis=0)
            k_out[m] = k_aug.T.astype(_BF16)

    def emit_v(v_t, n_heads, v_dim, v_out):
        v_bf = v_t.astype(_BF16)
        pad_rows = ones_then_zeros(v_out.shape[2] - v_dim)
        for hd in range(n_heads):
            for b in range(rows // vb):
                v_out[hd, b, 0:v_dim, :] = v_bf[hd * v_dim:(hd + 1) * v_dim, b * vb:(b + 1) * vb]
                v_out[hd, b, v_dim:, :] = pad_rows[:, b * vb:(b + 1) * vb]

    o = 0
    fq_t = proj_t(o, o + FOX_WIDTH); o += FOX_WIDTH
    fk_t = proj_t(o, o + FOX_WIDTH); o += FOX_WIDTH
    emit_qk(fq_t, fk_t, c_fox, fq_ref, fk_ref)
    fv_t = proj_t(o, o + FOX_WIDTH); o += FOX_WIDTH
    emit_v(fv_t, FOX_HEADS, HEAD_DIM, fv_ref)
    dq_t = proj_t(o, o + DIFF_WIDTH); o += DIFF_WIDTH
    dk_t = proj_t(o, o + DIFF_WIDTH); o += DIFF_WIDTH
    emit_qk(dq_t, dk_t, c_alibi, dq_ref, dk_ref)
    dv_t = proj_t(o, o + DIFF_WIDTH); o += DIFF_WIDTH
    emit_v(dv_t, DIFF_HEADS, DIFF_V_DIM, dv_ref)


def _projection(x2, g_pre_mix, w_in, b_forget):
    seq = x2.shape[0]
    rows = PROJ_ROWS
    n_t = seq // rows
    nvb = seq // V_BLOCK
    vb_per_t = rows // V_BLOCK
    f0 = 3 * FOX_WIDTH
    w_main = jnp.concatenate([w_in[:, :f0], w_in[:, f0 + FOX_HEADS:]], axis=1)
    w_f = jnp.pad(w_in[:, f0:f0 + FOX_HEADS], ((0, 0), (0, 16 - FOX_HEADS)))
    w_t = jnp.concatenate([w_main, w_f], axis=1).T.astype(_BF16)
    n_feat = w_t.shape[0]
    slopes = 2.0 ** (-8.0 * np.arange(1, DIFF_HEADS + 1) / DIFF_HEADS)
    slope_col = jnp.asarray(np.repeat(slopes, 2).reshape(DIFF_MAPS, 1), _F32)
    gq, gk = _bias_selectors()

    const2 = lambda t: (0, 0)
    out_shapes = (
        jax.ShapeDtypeStruct((FOX_HEADS, QK_FEATS, seq), _BF16),
        jax.ShapeDtypeStruct((FOX_HEADS, seq, QK_FEATS), _BF16),
        jax.ShapeDtypeStruct((FOX_HEADS, nvb, FOX_V_ROWS, V_BLOCK), _BF16),
        jax.ShapeDtypeStruct((DIFF_MAPS, QK_FEATS, seq), _BF16),
        jax.ShapeDtypeStruct((DIFF_MAPS, seq, QK_FEATS), _BF16),
        jax.ShapeDtypeStruct((DIFF_HEADS, nvb, DIFF_V_ROWS, V_BLOCK), _BF16),
    )
    out_specs = (
        pl.BlockSpec((FOX_HEADS, QK_FEATS, rows), lambda t: (0, 0, t)),
        pl.BlockSpec((FOX_HEADS, rows, QK_FEATS), lambda t: (0, t, 0)),
        pl.BlockSpec((FOX_HEADS, vb_per_t, FOX_V_ROWS, V_BLOCK), lambda t: (0, t, 0, 0)),
        pl.BlockSpec((DIFF_MAPS, QK_FEATS, rows), lambda t: (0, 0, t)),
        pl.BlockSpec((DIFF_MAPS, rows, QK_FEATS), lambda t: (0, t, 0)),
        pl.BlockSpec((DIFF_HEADS, vb_per_t, DIFF_V_ROWS, V_BLOCK), lambda t: (0, t, 0, 0)),
    )
    return pl.pallas_call(
        _proj_kernel,
        grid=(n_t,),
        in_specs=[
            pl.BlockSpec((rows, D_MODEL), lambda t: (t, 0)),
            pl.BlockSpec((1, D_MODEL), const2),
            pl.BlockSpec((n_feat, D_MODEL), const2),
            pl.BlockSpec((FOX_HEADS, 1), const2),
            pl.BlockSpec((DIFF_MAPS, 1), const2),
            pl.BlockSpec(gq.shape, const2),
            pl.BlockSpec(gk.shape, const2),
        ],
        out_specs=out_specs,
        out_shape=out_shapes,
        scratch_shapes=[pltpu.VMEM((FOX_HEADS, 128), _F32)],
        compiler_params=pltpu.CompilerParams(
            dimension_semantics=("arbitrary",), vmem_limit_bytes=VMEM_LIMIT_BYTES),
        name="proj",
    )(x2, g_pre_mix.reshape(1, D_MODEL), w_t, b_forget.reshape(FOX_HEADS, 1), slope_col, gq, gk)


def _flash_chains(chains, k_ref, v_ref, qi, s0_ref, s1_ref, m_ref, acc_ref, bmax_ref):
    blk = ATT_BLOCK
    s_refs = (s0_ref, s1_ref)
    n_full = qi
    n_chunk = blk // ROW_CHUNK
    n_tile = blk // PV_TILE
    chunks_per_tile = PV_TILE // ROW_CHUNK

    def chunk_max(s_c):
        return jnp.max(s_c.reshape(ROW_CHUNK // 8, 8, s_c.shape[-1]), axis=0)

    def park_chunk(chain, j, c, slots):
        q_t, k_idx, _ = chain
        start = pl.multiple_of(j * blk + c * ROW_CHUNK, ROW_CHUNK)
        s_c = jnp.dot(k_ref[k_idx, pl.ds(start, ROW_CHUNK), :], q_t, preferred_element_type=_F32)
        for slot in slots:
            s_refs[slot][c * ROW_CHUNK:(c + 1) * ROW_CHUNK, :] = s_c
        return chunk_max(s_c)

    def diag_scores(src, c):
        lo = c * ROW_CHUNK
        s_c = s_refs[src][lo:lo + ROW_CHUNK, lo:]
        row = lax.broadcasted_iota(jnp.int32, s_c.shape, 0)
        col = lax.broadcasted_iota(jnp.int32, s_c.shape, 1)
        return jnp.where(row <= col, s_c, MASK_VALUE)

    def step(chain, i, src, diagonal=False, park=None):
        m = m_ref[...]
        bmax = bmax_ref[...]
        v_idx = chain[2]
        if diagonal:
            parts = [chunk_max(diag_scores(src, c)) for c in range(n_chunk)]
            bmax = jnp.concatenate(
                [functools.reduce(jnp.maximum,
                                  [parts[c][:, (b - c) * ROW_CHUNK:(b - c + 1) * ROW_CHUNK]
                                   for c in range(b + 1)])
                 for b in range(n_chunk)], axis=1)
        m_new = jnp.maximum(m, jnp.max(bmax, axis=0, keepdims=True))
        alpha = jnp.exp2(m - m_new)
        pv_sum = [None] * n_tile
        next_max = None
        p_parts = []
        for c in range(n_chunk):
            lo = c * ROW_CHUNK
            if diagonal:
                p_c = jnp.exp2((diag_scores(src, c) - m_new[:, lo:]).astype(_BF16))
                if lo:
                    p_c = jnp.concatenate([jnp.zeros((ROW_CHUNK, lo), _BF16), p_c], axis=1)
            else:
                p_c = jnp.exp2((s_refs[src][lo:lo + ROW_CHUNK, :] - m_new).astype(_BF16))
            p_parts.append(p_c)
            if park is not None:
                cm = park_chunk(park[0], park[1], c, park[2])
                next_max = cm if next_max is None else jnp.maximum(next_max, cm)
            if len(p_parts) == chunks_per_tile:
                kt = lo // PV_TILE
                p_tile = jnp.concatenate(p_parts, axis=0)
                p_parts = []
                k0 = kt * PV_TILE
                v_tile = v_ref[v_idx, i * (blk // V_BLOCK) + k0 // V_BLOCK, :,
                               k0 % V_BLOCK:k0 % V_BLOCK + PV_TILE]
                for nt in range(kt if diagonal else 0, n_tile):
                    piece = jnp.dot(v_tile, p_tile[:, nt * PV_TILE:(nt + 1) * PV_TILE],
                                    preferred_element_type=_F32)
                    pv_sum[nt] = piece if pv_sum[nt] is None else pv_sum[nt] + piece
        acc_ref[...] = alpha * acc_ref[...] + jnp.concatenate(pv_sum, axis=1)
        m_ref[...] = m_new
        if park is not None:
            bmax_ref[...] = next_max

    bmax_ref[...] = functools.reduce(
        jnp.maximum, [park_chunk(chains[0], 0, c, (0, 1)) for c in range(n_chunk)])
    accs = []
    for n, chain in enumerate(chains):
        m_ref[...] = jnp.full(m_ref.shape, MASK_VALUE, _F32)
        acc_ref[...] = jnp.zeros_like(acc_ref)

        def trip(i, _, chain=chain):
            lax.cond(((i + n_full) & 1) == 0,
                     lambda: step(chain, i, 0, park=(chain, i + 1, (1,))),
                     lambda: step(chain, i, 1, park=(chain, i + 1, (0,))))
            return 0

        lax.fori_loop(0, n_full, trip, 0)
        park = (chains[n + 1], 0, (0, 1)) if n + 1 < len(chains) else None
        step(chain, n_full, 0, diagonal=True, park=park)
        accs.append(acc_ref[...])
    return accs


def _fox_kernel(q_ref, k_ref, v_ref, o_ref, *scratch):
    accs = _flash_chains([(q_ref[hh], hh, hh) for hh in range(2)], k_ref, v_ref,
                         pl.program_id(1), *scratch)
    outs = [acc[0:HEAD_DIM] / acc[HEAD_DIM:HEAD_DIM + 1] for acc in accs]
    o_t = jnp.concatenate(outs, axis=0)
    o_ref[...] = o_t.T.astype(o_ref.dtype)


def _diff_kernel(q_ref, k_ref, v_ref, lam_ref, g_ref, o_ref, *scratch):
    accs = _flash_chains([(q_ref[n], n, 0) for n in range(2)], k_ref, v_ref,
                         pl.program_id(1), *scratch)
    outs = [acc[0:DIFF_V_DIM] / acc[DIFF_V_DIM:DIFF_V_DIM + 1] for acc in accs]
    lp = lam_ref[...]
    lam = (jnp.exp(jnp.sum(lp[0:1] * lp[1:2], axis=1, keepdims=True))
           - jnp.exp(jnp.sum(lp[2:3] * lp[3:4], axis=1, keepdims=True)) + LAM_INIT)
    o_t = outs[0] - lam * outs[1]
    o_t = ((o_t * _rms_scale(o_t, 0)) * g_ref[...]) * (1.0 - LAM_INIT)
    o_ref[...] = o_t.T.astype(o_ref.dtype)


def _attention(q_t, k, v_t, kernel_fn, maps_per_step, v_per_step, extra=(), name=None):
    n_maps, _, seq = q_t.shape
    nvb, v_rows = v_t.shape[1], v_t.shape[2]
    n_groups = n_maps // maps_per_step
    extra_specs = [pl.BlockSpec(e.shape, lambda g, qi: (0, 0)) for e in extra]
    return pl.pallas_call(
        kernel_fn,
        grid=(n_groups, seq // ATT_BLOCK),
        in_specs=[
            pl.BlockSpec((maps_per_step, QK_FEATS, ATT_BLOCK), lambda g, qi: (g, 0, qi)),
            pl.BlockSpec((maps_per_step, seq, QK_FEATS), lambda g, qi: (g, 0, 0)),
            pl.BlockSpec((v_per_step, nvb, v_rows, V_BLOCK), lambda g, qi: (g, 0, 0, 0)),
        ] + extra_specs,
        out_specs=pl.BlockSpec((ATT_BLOCK, 128), lambda g, qi: (qi, g)),
        out_shape=jax.ShapeDtypeStruct((seq, n_groups * 128), _BF16),
        scratch_shapes=[pltpu.VMEM((ATT_BLOCK, ATT_BLOCK), _F32),
                        pltpu.VMEM((ATT_BLOCK, ATT_BLOCK), _F32),
                        pltpu.VMEM((1, ATT_BLOCK), _F32),
                        pltpu.VMEM((v_rows, ATT_BLOCK), _F32),
                        pltpu.VMEM((8, ATT_BLOCK), _F32)],
        compiler_params=pltpu.CompilerParams(
            dimension_semantics=("arbitrary", "arbitrary"), vmem_limit_bytes=VMEM_LIMIT_BYTES),
        name=name,
    )(q_t, k, v_t, *extra)


def _out_kernel(x_ref, fox_ref, diff_ref, wo_ref, wu_ref, wd_ref, g_ref, o_ref):
    g = g_ref[...]
    mix = (jnp.dot(fox_ref[...], wo_ref[0:FOX_WIDTH, :], preferred_element_type=_F32)
           + jnp.dot(diff_ref[...], wo_ref[FOX_WIDTH:, :], preferred_element_type=_F32))
    x1 = x_ref[...] + (mix * _rms_scale(mix, -1)) * g[0:1]
    h = ((x1 * _rms_scale(x1, -1)) * g[1:2]).astype(_BF16)
    d = jnp.zeros_like(x1)
    for f in range(D_FF // FF_CHUNK):
        u = jnp.dot(h, wu_ref[:, f * FF_CHUNK:(f + 1) * FF_CHUNK], preferred_element_type=_F32)
        u = jnp.square(jnp.maximum(u, 0.0)).astype(_BF16)
        d = d + jnp.dot(u, wd_ref[f * FF_CHUNK:(f + 1) * FF_CHUNK, :], preferred_element_type=_F32)
    o_ref[...] = x1 + (d * _rms_scale(d, -1)) * g[2:3]


def _output_stage(x2, fox, diff, w_out, w_up, w_down, gains):
    seq = x2.shape[0]
    rows = OUT_ROWS
    const2 = lambda t: (0, 0)
    resident = functools.partial(pl.BlockSpec, index_map=const2, pipeline_mode=pl.Buffered(1))
    return pl.pallas_call(
        _out_kernel,
        grid=(seq // rows,),
        in_specs=[
            pl.BlockSpec((rows, D_MODEL), lambda t: (t, 0)),
            pl.BlockSpec((rows, FOX_WIDTH), lambda t: (t, 0)),
            pl.BlockSpec((rows, DIFF_WIDTH), lambda t: (t, 0)),
            resident((D_MODEL, D_MODEL)),
            resident((D_MODEL, D_FF)),
            resident((D_FF, D_MODEL)),
            pl.BlockSpec((3, D_MODEL), const2),
        ],
        out_specs=pl.BlockSpec((rows, D_MODEL), lambda t: (t, 0)),
        out_shape=jax.ShapeDtypeStruct((seq, D_MODEL), _F32),
        compiler_params=pltpu.CompilerParams(
            dimension_semantics=("arbitrary",), vmem_limit_bytes=VMEM_LIMIT_BYTES),
        name="out_mlp",
    )(x2, fox, diff, w_out, w_up, w_down, gains)


def kernel(x, g_pre_mix, w_in, b_forget, lambda_q1, lambda_k1, lambda_q2, lambda_k2,
           g_subln, w_out, g_post_mix, g_pre_mlp, w_up, w_down, g_post_mlp):
    batch, seq, _ = x.shape
    assert batch == 1 and w_in.shape[0] == 1
    assert seq % PROJ_ROWS == 0 and seq % ATT_BLOCK == 0 and seq % OUT_ROWS == 0
    assert PROJ_ROWS % V_BLOCK == 0 and ATT_BLOCK % V_BLOCK == 0
    assert V_BLOCK % PV_TILE == 0 and PV_TILE % ROW_CHUNK == 0
    x2 = x.reshape(seq, D_MODEL)
    fq_t, fk, fv_t, dq_t, dk, dv_t = _projection(x2, g_pre_mix[0], w_in[0], b_forget[0])
    fox = _attention(fq_t, fk, fv_t, _fox_kernel, 2, 2, name="fox_attn")
    lam_params = jnp.stack([lambda_q1[0], lambda_k1[0], lambda_q2[0], lambda_k2[0]]).astype(_F32)
    diff = _attention(dq_t, dk, dv_t, _diff_kernel, 2, 1,
                      extra=(lam_params, g_subln[0].reshape(DIFF_V_DIM, 1)), name="diff_attn")
    gains = jnp.stack([g_post_mix[0], g_pre_mlp[0], g_post_mlp[0]])
    out = _output_stage(x2, fox, diff, w_out[0].astype(_BF16), w_up[0].astype(_BF16),
                        w_down[0].astype(_BF16), gains)
    return out.reshape(batch, seq, D_MODEL)
```

```python
import functools
import math

import numpy as np
import jax
import jax.numpy as jnp
from jax import lax
from jax.experimental import pallas as pl
from jax.experimental.pallas import tpu as pltpu

D_MODEL = 1024
HEAD_DIM = 64
FOX_HEADS = 8
FOX_WIDTH = FOX_HEADS * HEAD_DIM
DIFF_HEADS = 4
DIFF_MAPS = 2 * DIFF_HEADS
DIFF_V_DIM = 2 * HEAD_DIM
DIFF_WIDTH = DIFF_HEADS * DIFF_V_DIM
D_FF = 4 * D_MODEL
RMS_EPS = 1e-6
LAM_INIT = 0.8 - 0.6 * math.exp(-0.3 * 0)
SCORE_SCALE = HEAD_DIM ** -0.5
LOG2E = math.log2(math.e)

QK_FEATS = 128
BIAS_ROWS = 16
FOX_V_ROWS = HEAD_DIM + BIAS_ROWS
DIFF_V_ROWS = DIFF_V_DIM + BIAS_ROWS
MASK_VALUE = -1e30

PROJ_ROWS = 512
ATT_BLOCK = 1024
ROW_CHUNK = 256
PV_TILE = 256
V_BLOCK = 512
OUT_ROWS = 512
FF_CHUNK = 1024
VMEM_LIMIT_BYTES = 56 * 1024 * 1024

_F32 = jnp.float32
_BF16 = jnp.bfloat16


def _split3(v):
    hi = v.astype(_BF16).astype(_F32)
    r = v - hi
    mid = r.astype(_BF16).astype(_F32)
    lo = (r - mid).astype(_BF16).astype(_F32)
    return hi, mid, lo


def _bias_selectors():
    gq = np.zeros((8 * BIAS_ROWS, 32), np.float32)
    gk = np.zeros((8 * BIAS_ROWS, 32), np.float32)
    for h in range(8):
        for part in range(3):
            gq[h * BIAS_ROWS + part, part * 8 + h] = 1.0
            gq[h * BIAS_ROWS + 3 + part, 24 + h] = 1.0
            gk[h * BIAS_ROWS + part, 24 + h] = 1.0
            gk[h * BIAS_ROWS + 3 + part, part * 8 + h] = -1.0
    return jnp.asarray(gq, _BF16), jnp.asarray(gk, _BF16)


def _rms_scale(v, axis):
    return lax.rsqrt(jnp.mean(v * v, axis=axis, keepdims=True) + RMS_EPS)


def _proj_kernel(x_ref, g_ref, w_ref, bf_ref, slope_ref, gq_ref, gk_ref,
                 fq_ref, fk_ref, fv_ref, dq_ref, dk_ref, dv_ref, carry_ref):
    t = pl.program_id(0)
    rows = x_ref.shape[0]

    @pl.when(t == 0)
    def _():
        carry_ref[...] = jnp.zeros_like(carry_ref)

    x = x_ref[...]
    h = ((x * _rms_scale(x, -1)) * g_ref[...]).astype(_BF16)

    def proj_t(lo, hi):
        return lax.dot_general(w_ref[lo:hi, :], h, (((1,), (1,)), ((), ())),
                               preferred_element_type=_F32)

    ones8 = jnp.ones((8, rows), _F32)

    def bias_feats(c):
        c_hi, c_mid, c_lo = _split3(c * LOG2E)
        c3 = jnp.concatenate([c_hi, c_mid, c_lo, ones8], axis=0).astype(_BF16)
        fq = jnp.dot(gq_ref[...], c3, preferred_element_type=_F32)
        fk = jnp.dot(gk_ref[...], c3, preferred_element_type=_F32)
        return fq, fk

    fl = proj_t(3 * FOX_WIDTH + 3 * DIFF_WIDTH, 3 * FOX_WIDTH + 3 * DIFF_WIDTH + 16)[0:8]
    fl = fl + bf_ref[...]
    ls = jnp.minimum(fl, 0.0) - jnp.log1p(jnp.exp(-jnp.abs(fl)))
    l_hi, l_mid, l_lo = _split3(ls)
    stack = jnp.concatenate([l_hi, l_mid, l_lo, jnp.zeros((8, rows), _F32)], axis=0).astype(_BF16)
    r_i = lax.broadcasted_iota(jnp.int32, (rows, rows), 0)
    c_i = lax.broadcasted_iota(jnp.int32, (rows, rows), 1)
    upper = (r_i <= c_i).astype(_BF16)
    cs = jnp.dot(stack, upper, preferred_element_type=_F32)
    c_fox = carry_ref[:, 0:1] + ((cs[0:8] + cs[8:16]) + cs[16:24])
    carry_ref[...] = jnp.broadcast_to(c_fox[:, rows - 1:rows], carry_ref.shape)

    pos = (lax.broadcasted_iota(jnp.int32, (8, rows), 1) + t * rows).astype(_F32)
    c_alibi = -(slope_ref[...] * pos)

    def ones_then_zeros(n):
        return (lax.broadcasted_iota(jnp.int32, (n, rows), 0) == 0).astype(_BF16)

    zero_pad_q = jnp.zeros((QK_FEATS - HEAD_DIM - BIAS_ROWS, rows), _BF16)
    zero_pad_k = jnp.zeros((QK_FEATS - HEAD_DIM - BIAS_ROWS, rows), _F32)
    vb = fv_ref.shape[-1]

    def emit_qk(q_t, k_t, c, q_out, k_out):
        feat_q, feat_k = bias_feats(c)
        for m in range(8):
            d0, b0 = m * HEAD_DIM, m * BIAS_ROWS
            q_out[m, 0:HEAD_DIM, :] = (q_t[d0:d0 + HEAD_DIM] * (SCORE_SCALE * LOG2E)).astype(_BF16)
            q_out[m, HEAD_DIM:HEAD_DIM + BIAS_ROWS, :] = feat_q[b0:b0 + BIAS_ROWS].astype(_BF16)
            q_out[m, HEAD_DIM + BIAS_ROWS:, :] = zero_pad_q
            k_aug = jnp.concatenate(
                [k_t[d0:d0 + HEAD_DIM], feat_k[b0:b0 + BIAS_ROWS], zero_pad_k], axis=0)
            k_out[m] = k_aug.T.astype(_BF16)

    def emit_v(v_t, n_heads, v_dim, v_out):
        v_bf = v_t.astype(_BF16)
        pad_rows = ones_then_zeros(v_out.shape[2] - v_dim)
        for hd in range(n_heads):
            for b in range(rows // vb):
                v_out[hd, b, 0:v_dim, :] = v_bf[hd * v_dim:(hd + 1) * v_dim, b * vb:(b + 1) * vb]
                v_out[hd, b, v_dim:, :] = pad_rows[:, b * vb:(b + 1) * vb]

    o = 0
    fq_t = proj_t(o, o + FOX_WIDTH); o += FOX_WIDTH
    fk_t = proj_t(o, o + FOX_WIDTH); o += FOX_WIDTH
    emit_qk(fq_t, fk_t, c_fox, fq_ref, fk_ref)
    fv_t = proj_t(o, o + FOX_WIDTH); o += FOX_WIDTH
    emit_v(fv_t, FOX_HEADS, HEAD_DIM, fv_ref)
    dq_t = proj_t(o, o + DIFF_WIDTH); o += DIFF_WIDTH
    dk_t = proj_t(o, o + DIFF_WIDTH); o += DIFF_WIDTH
    emit_qk(dq_t, dk_t, c_alibi, dq_ref, dk_ref)
    dv_t = proj_t(o, o + DIFF_WIDTH); o += DIFF_WIDTH
    emit_v(dv_t, DIFF_HEADS, DIFF_V_DIM, dv_ref)


def _projection(x2, g_pre_mix, w_in, b_forget):
    seq = x2.shape[0]
    rows = PROJ_ROWS
    n_t = seq // rows
    nvb = seq // V_BLOCK
    vb_per_t = rows // V_BLOCK
    f0 = 3 * FOX_WIDTH
    w_main = jnp.concatenate([w_in[:, :f0], w_in[:, f0 + FOX_HEADS:]], axis=1)
    w_f = jnp.pad(w_in[:, f0:f0 + FOX_HEADS], ((0, 0), (0, 16 - FOX_HEADS)))
    w_t = jnp.concatenate([w_main, w_f], axis=1).T.astype(_BF16)
    n_feat = w_t.shape[0]
    slopes = 2.0 ** (-8.0 * np.arange(1, DIFF_HEADS + 1) / DIFF_HEADS)
    slope_col = jnp.asarray(np.repeat(slopes, 2).reshape(DIFF_MAPS, 1), _F32)
    gq, gk = _bias_selectors()

    const2 = lambda t: (0, 0)
    out_shapes = (
        jax.ShapeDtypeStruct((FOX_HEADS, QK_FEATS, seq), _BF16),
        jax.ShapeDtypeStruct((FOX_HEADS, seq, QK_FEATS), _BF16),
        jax.ShapeDtypeStruct((FOX_HEADS, nvb, FOX_V_ROWS, V_BLOCK), _BF16),
        jax.ShapeDtypeStruct((DIFF_MAPS, QK_FEATS, seq), _BF16),
        jax.ShapeDtypeStruct((DIFF_MAPS, seq, QK_FEATS), _BF16),
        jax.ShapeDtypeStruct((DIFF_HEADS, nvb, DIFF_V_ROWS, V_BLOCK), _BF16),
    )
    out_specs = (
        pl.BlockSpec((FOX_HEADS, QK_FEATS, rows), lambda t: (0, 0, t)),
        pl.BlockSpec((FOX_HEADS, rows, QK_FEATS), lambda t: (0, t, 0)),
        pl.BlockSpec((FOX_HEADS, vb_per_t, FOX_V_ROWS, V_BLOCK), lambda t: (0, t, 0, 0)),
        pl.BlockSpec((DIFF_MAPS, QK_FEATS, rows), lambda t: (0, 0, t)),
        pl.BlockSpec((DIFF_MAPS, rows, QK_FEATS), lambda t: (0, t, 0)),
        pl.BlockSpec((DIFF_HEADS, vb_per_t, DIFF_V_ROWS, V_BLOCK), lambda t: (0, t, 0, 0)),
    )
    return pl.pallas_call(
        _proj_kernel,
        grid=(n_t,),
        in_specs=[
            pl.BlockSpec((rows, D_MODEL), lambda t: (t, 0)),
            pl.BlockSpec((1, D_MODEL), const2),
            pl.BlockSpec((n_feat, D_MODEL), const2),
            pl.BlockSpec((FOX_HEADS, 1), const2),
            pl.BlockSpec((DIFF_MAPS, 1), const2),
            pl.BlockSpec(gq.shape, const2),
            pl.BlockSpec(gk.shape, const2),
        ],
        out_specs=out_specs,
        out_shape=out_shapes,
        scratch_shapes=[pltpu.VMEM((FOX_HEADS, 128), _F32)],
        compiler_params=pltpu.CompilerParams(
            dimension_semantics=("arbitrary",), vmem_limit_bytes=VMEM_LIMIT_BYTES),
        name="proj",
    )(x2, g_pre_mix.reshape(1, D_MODEL), w_t, b_forget.reshape(FOX_HEADS, 1), slope_col, gq, gk)


def _flash_chains(chains, k_ref, v_ref, qi, scratch):
    blk = ATT_BLOCK
    n_full = qi
    n_chunk = blk // ROW_CHUNK
    n_tile = blk // PV_TILE
    chunks_per_tile = PV_TILE // ROW_CHUNK
    n_map = len(chains)
    s_refs = [scratch[5 * n:5 * n + 2] for n in range(n_map)]
    m_refs = [scratch[5 * n + 2] for n in range(n_map)]
    acc_refs = [scratch[5 * n + 3] for n in range(n_map)]
    bmax_refs = [scratch[5 * n + 4] for n in range(n_map)]

    def chunk_max(s_c):
        return jnp.max(s_c.reshape(ROW_CHUNK // 8, 8, s_c.shape[-1]), axis=0)

    def park_chunk(n, j, c, slots):
        q_t, k_idx, _ = chains[n]
        start = pl.multiple_of(j * blk + c * ROW_CHUNK, ROW_CHUNK)
        s_c = jnp.dot(k_ref[k_idx, pl.ds(start, ROW_CHUNK), :], q_t, preferred_element_type=_F32)
        for slot in slots:
            s_refs[n][slot][c * ROW_CHUNK:(c + 1) * ROW_CHUNK, :] = s_c
        return chunk_max(s_c)

    def diag_scores(n, src, c):
        lo = c * ROW_CHUNK
        s_c = s_refs[n][src][lo:lo + ROW_CHUNK, lo:]
        row = lax.broadcasted_iota(jnp.int32, s_c.shape, 0)
        col = lax.broadcasted_iota(jnp.int32, s_c.shape, 1)
        return jnp.where(row <= col, s_c, MASK_VALUE)

    def step(i, src, diagonal=False):
        m_new, alpha = [], []
        for n in range(n_map):
            bmax = bmax_refs[n][...]
            if diagonal:
                parts = [chunk_max(diag_scores(n, src, c)) for c in range(n_chunk)]
                bmax = jnp.concatenate(
                    [functools.reduce(jnp.maximum,
                                      [parts[c][:, (b - c) * ROW_CHUNK:(b - c + 1) * ROW_CHUNK]
                                       for c in range(b + 1)])
                     for b in range(n_chunk)], axis=1)
            m = m_refs[n][...]
            m_new.append(jnp.maximum(m, jnp.max(bmax, axis=0, keepdims=True)))
            alpha.append(jnp.exp2(m - m_new[n]))
        pv_sum = [[None] * n_tile for _ in range(n_map)]
        next_max = [None] * n_map
        p_parts = [[] for _ in range(n_map)]
        for c in range(n_chunk):
            lo = c * ROW_CHUNK
            for n in range(n_map):
                if diagonal:
                    p_c = jnp.exp2((diag_scores(n, src, c) - m_new[n][:, lo:]).astype(_BF16))
                    if lo:
                        p_c = jnp.concatenate([jnp.zeros((ROW_CHUNK, lo), _BF16), p_c], axis=1)
                else:
                    p_c = jnp.exp2(
                        (s_refs[n][src][lo:lo + ROW_CHUNK, :] - m_new[n]).astype(_BF16))
                p_parts[n].append(p_c)
                if not diagonal:
                    cm = park_chunk(n, i + 1, c, (1 - src,))
                    next_max[n] = cm if next_max[n] is None else jnp.maximum(next_max[n], cm)
                if len(p_parts[n]) == chunks_per_tile:
                    kt = lo // PV_TILE
                    p_tile = jnp.concatenate(p_parts[n], axis=0)
                    p_parts[n] = []
                    k0 = kt * PV_TILE
                    v_tile = v_ref[chains[n][2], i * (blk // V_BLOCK) + k0 // V_BLOCK, :,
                                   k0 % V_BLOCK:k0 % V_BLOCK + PV_TILE]
                    for nt in range(kt if diagonal else 0, n_tile):
                        piece = jnp.dot(v_tile, p_tile[:, nt * PV_TILE:(nt + 1) * PV_TILE],
                                        preferred_element_type=_F32)
                        pv_sum[n][nt] = piece if pv_sum[n][nt] is None else pv_sum[n][nt] + piece
        for n in range(n_map):
            acc_refs[n][...] = alpha[n] * acc_refs[n][...] + jnp.concatenate(pv_sum[n], axis=1)
            m_refs[n][...] = m_new[n]
            if not diagonal:
                bmax_refs[n][...] = next_max[n]

    for n in range(n_map):
        bmax_refs[n][...] = functools.reduce(
            jnp.maximum, [park_chunk(n, 0, c, (0, 1)) for c in range(n_chunk)])
        m_refs[n][...] = jnp.full(m_refs[n].shape, MASK_VALUE, _F32)
        acc_refs[n][...] = jnp.zeros_like(acc_refs[n])

    def trip(i, _):
        lax.cond(((i + n_full) & 1) == 0, lambda: step(i, 0), lambda: step(i, 1))
        return 0

    lax.fori_loop(0, n_full, trip, 0)
    step(n_full, 0, diagonal=True)
    return [acc_refs[n][...] for n in range(n_map)]


def _fox_kernel(q_ref, k_ref, v_ref, o_ref, *scratch):
    accs = _flash_chains([(q_ref[hh], hh, hh) for hh in range(2)], k_ref, v_ref,
                         pl.program_id(1), scratch)
    outs = [acc[0:HEAD_DIM] / acc[HEAD_DIM:HEAD_DIM + 1] for acc in accs]
    o_t = jnp.concatenate(outs, axis=0)
    o_ref[...] = o_t.T.astype(o_ref.dtype)


def _diff_kernel(q_ref, k_ref, v_ref, lam_ref, g_ref, o_ref, *scratch):
    accs = _flash_chains([(q_ref[n], n, 0) for n in range(2)], k_ref, v_ref,
                         pl.program_id(1), scratch)
    outs = [acc[0:DIFF_V_DIM] / acc[DIFF_V_DIM:DIFF_V_DIM + 1] for acc in accs]
    lp = lam_ref[...]
    lam = (jnp.exp(jnp.sum(lp[0:1] * lp[1:2], axis=1, keepdims=True))
           - jnp.exp(jnp.sum(lp[2:3] * lp[3:4], axis=1, keepdims=True)) + LAM_INIT)
    o_t = outs[0] - lam * outs[1]
    o_t = ((o_t * _rms_scale(o_t, 0)) * g_ref[...]) * (1.0 - LAM_INIT)
    o_ref[...] = o_t.T.astype(o_ref.dtype)


def _attention(q_t, k, v_t, kernel_fn, maps_per_step, v_per_step, extra=(), name=None):
    n_maps, _, seq = q_t.shape
    nvb, v_rows = v_t.shape[1], v_t.shape[2]
    n_groups = n_maps // maps_per_step
    extra_specs = [pl.BlockSpec(e.shape, lambda g, qi: (0, 0)) for e in extra]
    return pl.pallas_call(
        kernel_fn,
        grid=(n_groups, seq // ATT_BLOCK),
        in_specs=[
            pl.BlockSpec((maps_per_step, QK_FEATS, ATT_BLOCK), lambda g, qi: (g, 0, qi)),
            pl.BlockSpec((maps_per_step, seq, QK_FEATS), lambda g, qi: (g, 0, 0)),
            pl.BlockSpec((v_per_step, nvb, v_rows, V_BLOCK), lambda g, qi: (g, 0, 0, 0)),
        ] + extra_specs,
        out_specs=pl.BlockSpec((ATT_BLOCK, 128), lambda g, qi: (qi, g)),
        out_shape=jax.ShapeDtypeStruct((seq, n_groups * 128), _BF16),
        scratch_shapes=maps_per_step * [
            pltpu.VMEM((ATT_BLOCK, ATT_BLOCK), _F32),
            pltpu.VMEM((ATT_BLOCK, ATT_BLOCK), _F32),
            pltpu.VMEM((1, ATT_BLOCK), _F32),
            pltpu.VMEM((v_rows, ATT_BLOCK), _F32),
            pltpu.VMEM((8, ATT_BLOCK), _F32)],
        compiler_params=pltpu.CompilerParams(
            dimension_semantics=("arbitrary", "arbitrary"), vmem_limit_bytes=VMEM_LIMIT_BYTES),
        name=name,
    )(q_t, k, v_t, *extra)


def _out_kernel(x_ref, fox_ref, diff_ref, wo_ref, wu_ref, wd_ref, g_ref, o_ref):
    g = g_ref[...]
    mix = (jnp.dot(fox_ref[...], wo_ref[0:FOX_WIDTH, :], preferred_element_type=_F32)
           + jnp.dot(diff_ref[...], wo_ref[FOX_WIDTH:, :], preferred_element_type=_F32))
    x1 = x_ref[...] + (mix * _rms_scale(mix, -1)) * g[0:1]
    h = ((x1 * _rms_scale(x1, -1)) * g[1:2]).astype(_BF16)
    d = jnp.zeros_like(x1)
    for f in range(D_FF // FF_CHUNK):
        u = jnp.dot(h, wu_ref[:, f * FF_CHUNK:(f + 1) * FF_CHUNK], preferred_element_type=_F32)
        u = jnp.square(jnp.maximum(u, 0.0)).astype(_BF16)
        d = d + jnp.dot(u, wd_ref[f * FF_CHUNK:(f + 1) * FF_CHUNK, :], preferred_element_type=_F32)
    o_ref[...] = x1 + (d * _rms_scale(d, -1)) * g[2:3]


def _output_stage(x2, fox, diff, w_out, w_up, w_down, gains):
    seq = x2.shape[0]
    rows = OUT_ROWS
    const2 = lambda t: (0, 0)
    resident = functools.partial(pl.BlockSpec, index_map=const2, pipeline_mode=pl.Buffered(1))
    return pl.pallas_call(
        _out_kernel,
        grid=(seq // rows,),
        in_specs=[
            pl.BlockSpec((rows, D_MODEL), lambda t: (t, 0)),
            pl.BlockSpec((rows, FOX_WIDTH), lambda t: (t, 0)),
            pl.BlockSpec((rows, DIFF_WIDTH), lambda t: (t, 0)),
            resident((D_MODEL, D_MODEL)),
            resident((D_MODEL, D_FF)),
            resident((D_FF, D_MODEL)),
            pl.BlockSpec((3, D_MODEL), const2),
        ],
        out_specs=pl.BlockSpec((rows, D_MODEL), lambda t: (t, 0)),
        out_shape=jax.ShapeDtypeStruct((seq, D_MODEL), _F32),
        compiler_params=pltpu.CompilerParams(
            dimension_semantics=("arbitrary",), vmem_limit_bytes=VMEM_LIMIT_BYTES),
        name="out_mlp",
    )(x2, fox, diff, w_out, w_up, w_down, gains)


def kernel(x, g_pre_mix, w_in, b_forget, lambda_q1, lambda_k1, lambda_q2, lambda_k2,
           g_subln, w_out, g_post_mix, g_pre_mlp, w_up, w_down, g_post_mlp):
    batch, seq, _ = x.shape
    assert batch == 1 and w_in.shape[0] == 1
    assert seq % PROJ_ROWS == 0 and seq % ATT_BLOCK == 0 and seq % OUT_ROWS == 0
    assert PROJ_ROWS % V_BLOCK == 0 and ATT_BLOCK % V_BLOCK == 0
    assert V_BLOCK % PV_TILE == 0 and PV_TILE % ROW_CHUNK == 0
    x2 = x.reshape(seq, D_MODEL)
    fq_t, fk, fv_t, dq_t, dk, dv_t = _projection(x2, g_pre_mix[0], w_in[0], b_forget[0])
    fox = _attention(fq_t, fk, fv_t, _fox_kernel, 2, 2, name="fox_attn")
    lam_params = jnp.stack([lambda_q1[0], lambda_k1[0], lambda_q2[0], lambda_k2[0]]).astype(_F32)
    diff = _attention(dq_t, dk, dv_t, _diff_kernel, 2, 1,
                      extra=(lam_params, g_subln[0].reshape(DIFF_V_DIM, 1)), name="diff_attn")
    gains = jnp.stack([g_post_mix[0], g_pre_mlp[0], g_post_mlp[0]])
    out = _output_stage(x2, fox, diff, w_out[0].astype(_BF16), w_up[0].astype(_BF16),
                        w_down[0].astype(_BF16), gains)
    return out.reshape(batch, seq, D_MODEL)
```

```python
import functools
import math

import numpy as np
import jax
import jax.numpy as jnp
from jax import lax
from jax.experimental import pallas as pl
from jax.experimental.pallas import tpu as pltpu

D_MODEL = 1024
HEAD_DIM = 64
FOX_HEADS = 8
FOX_WIDTH = FOX_HEADS * HEAD_DIM
DIFF_HEADS = 4
DIFF_MAPS = 2 * DIFF_HEADS
DIFF_V_DIM = 2 * HEAD_DIM
DIFF_WIDTH = DIFF_HEADS * DIFF_V_DIM
D_FF = 4 * D_MODEL
RMS_EPS = 1e-6
LAM_INIT = 0.8 - 0.6 * math.exp(-0.3 * 0)
SCORE_SCALE = HEAD_DIM ** -0.5
LOG2E = math.log2(math.e)

QK_FEATS = 128
BIAS_ROWS = 16
FOX_V_ROWS = HEAD_DIM + BIAS_ROWS
DIFF_V_ROWS = DIFF_V_DIM + BIAS_ROWS
MASK_VALUE = -1e30

PROJ_ROWS = 512
ATT_BLOCK = 1024
ATT_MAPS = 4
ROW_CHUNK = 256
PV_TILE = 256
V_BLOCK = 512
OUT_ROWS = 512
FF_CHUNK = 1024
VMEM_LIMIT_BYTES = 56 * 1024 * 1024

_F32 = jnp.float32
_BF16 = jnp.bfloat16


def _split3(v):
    hi = v.astype(_BF16).astype(_F32)
    r = v - hi
    mid = r.astype(_BF16).astype(_F32)
    lo = (r - mid).astype(_BF16).astype(_F32)
    return hi, mid, lo


def _bias_selectors():
    gq = np.zeros((8 * BIAS_ROWS, 32), np.float32)
    gk = np.zeros((8 * BIAS_ROWS, 32), np.float32)
    for h in range(8):
        for part in range(3):
            gq[h * BIAS_ROWS + part, part * 8 + h] = 1.0
            gq[h * BIAS_ROWS + 3 + part, 24 + h] = 1.0
            gk[h * BIAS_ROWS + part, 24 + h] = 1.0
            gk[h * BIAS_ROWS + 3 + part, part * 8 + h] = -1.0
    return jnp.asarray(gq, _BF16), jnp.asarray(gk, _BF16)


def _rms_scale(v, axis):
    return lax.rsqrt(jnp.mean(v * v, axis=axis, keepdims=True) + RMS_EPS)


def _proj_kernel(x_ref, g_ref, w_ref, bf_ref, slope_ref, gq_ref, gk_ref,
                 fq_ref, fk_ref, fv_ref, dq_ref, dk_ref, dv_ref, carry_ref):
    t = pl.program_id(0)
    rows = x_ref.shape[0]

    @pl.when(t == 0)
    def _():
        carry_ref[...] = jnp.zeros_like(carry_ref)

    x = x_ref[...]
    h = ((x * _rms_scale(x, -1)) * g_ref[...]).astype(_BF16)

    def proj_t(lo, hi):
        return lax.dot_general(w_ref[lo:hi, :], h, (((1,), (1,)), ((), ())),
                               preferred_element_type=_F32)

    ones8 = jnp.ones((8, rows), _F32)

    def bias_feats(c):
        c_hi, c_mid, c_lo = _split3(c * LOG2E)
        c3 = jnp.concatenate([c_hi, c_mid, c_lo, ones8], axis=0).astype(_BF16)
        fq = jnp.dot(gq_ref[...], c3, preferred_element_type=_F32)
        fk = jnp.dot(gk_ref[...], c3, preferred_element_type=_F32)
        return fq, fk

    fl = proj_t(3 * FOX_WIDTH + 3 * DIFF_WIDTH, 3 * FOX_WIDTH + 3 * DIFF_WIDTH + 16)[0:8]
    fl = fl + bf_ref[...]
    ls = jnp.minimum(fl, 0.0) - jnp.log1p(jnp.exp(-jnp.abs(fl)))
    l_hi, l_mid, l_lo = _split3(ls)
    stack = jnp.concatenate([l_hi, l_mid, l_lo, jnp.zeros((8, rows), _F32)], axis=0).astype(_BF16)
    r_i = lax.broadcasted_iota(jnp.int32, (rows, rows), 0)
    c_i = lax.broadcasted_iota(jnp.int32, (rows, rows), 1)
    upper = (r_i <= c_i).astype(_BF16)
    cs = jnp.dot(stack, upper, preferred_element_type=_F32)
    c_fox = carry_ref[:, 0:1] + ((cs[0:8] + cs[8:16]) + cs[16:24])
    carry_ref[...] = jnp.broadcast_to(c_fox[:, rows - 1:rows], carry_ref.shape)

    pos = (lax.broadcasted_iota(jnp.int32, (8, rows), 1) + t * rows).astype(_F32)
    c_alibi = -(slope_ref[...] * pos)

    def ones_then_zeros(n):
        return (lax.broadcasted_iota(jnp.int32, (n, rows), 0) == 0).astype(_BF16)

    zero_pad_q = jnp.zeros((QK_FEATS - HEAD_DIM - BIAS_ROWS, rows), _BF16)
    zero_pad_k = jnp.zeros((QK_FEATS - HEAD_DIM - BIAS_ROWS, rows), _F32)
    vb = fv_ref.shape[-1]

    def emit_qk(q_t, k_t, c, q_out, k_out):
        feat_q, feat_k = bias_feats(c)
        for m in range(8):
            d0, b0 = m * HEAD_DIM, m * BIAS_ROWS
            q_out[m, 0:HEAD_DIM, :] = (q_t[d0:d0 + HEAD_DIM] * (SCORE_SCALE * LOG2E)).astype(_BF16)
            q_out[m, HEAD_DIM:HEAD_DIM + BIAS_ROWS, :] = feat_q[b0:b0 + BIAS_ROWS].astype(_BF16)
            q_out[m, HEAD_DIM + BIAS_ROWS:, :] = zero_pad_q
            k_aug = jnp.concatenate(
                [k_t[d0:d0 + HEAD_DIM], feat_k[b0:b0 + BIAS_ROWS], zero_pad_k], axis=0)
            k_out[m] = k_aug.T.astype(_BF16)

    def emit_v(v_t, n_heads, v_dim, v_out):
        v_bf = v_t.astype(_BF16)
        pad_rows = ones_then_zeros(v_out.shape[2] - v_dim)
        for hd in range(n_heads):
            for b in range(rows // vb):
                v_out[hd, b, 0:v_dim, :] = v_bf[hd * v_dim:(hd + 1) * v_dim, b * vb:(b + 1) * vb]
                v_out[hd, b, v_dim:, :] = pad_rows[:, b * vb:(b + 1) * vb]

    o = 0
    fq_t = proj_t(o, o + FOX_WIDTH); o += FOX_WIDTH
    fk_t = proj_t(o, o + FOX_WIDTH); o += FOX_WIDTH
    emit_qk(fq_t, fk_t, c_fox, fq_ref, fk_ref)
    fv_t = proj_t(o, o + FOX_WIDTH); o += FOX_WIDTH
    emit_v(fv_t, FOX_HEADS, HEAD_DIM, fv_ref)
    dq_t = proj_t(o, o + DIFF_WIDTH); o += DIFF_WIDTH
    dk_t = proj_t(o, o + DIFF_WIDTH); o += DIFF_WIDTH
    emit_qk(dq_t, dk_t, c_alibi, dq_ref, dk_ref)
    dv_t = proj_t(o, o + DIFF_WIDTH); o += DIFF_WIDTH
    emit_v(dv_t, DIFF_HEADS, DIFF_V_DIM, dv_ref)


def _projection(x2, g_pre_mix, w_in, b_forget):
    seq = x2.shape[0]
    rows = PROJ_ROWS
    n_t = seq // rows
    nvb = seq // V_BLOCK
    vb_per_t = rows // V_BLOCK
    f0 = 3 * FOX_WIDTH
    w_main = jnp.concatenate([w_in[:, :f0], w_in[:, f0 + FOX_HEADS:]], axis=1)
    w_f = jnp.pad(w_in[:, f0:f0 + FOX_HEADS], ((0, 0), (0, 16 - FOX_HEADS)))
    w_t = jnp.concatenate([w_main, w_f], axis=1).T.astype(_BF16)
    n_feat = w_t.shape[0]
    slopes = 2.0 ** (-8.0 * np.arange(1, DIFF_HEADS + 1) / DIFF_HEADS)
    slope_col = jnp.asarray(np.repeat(slopes, 2).reshape(DIFF_MAPS, 1), _F32)
    gq, gk = _bias_selectors()

    const2 = lambda t: (0, 0)
    out_shapes = (
        jax.ShapeDtypeStruct((FOX_HEADS, QK_FEATS, seq), _BF16),
        jax.ShapeDtypeStruct((FOX_HEADS, seq, QK_FEATS), _BF16),
        jax.ShapeDtypeStruct((FOX_HEADS, nvb, FOX_V_ROWS, V_BLOCK), _BF16),
        jax.ShapeDtypeStruct((DIFF_MAPS, QK_FEATS, seq), _BF16),
        jax.ShapeDtypeStruct((DIFF_MAPS, seq, QK_FEATS), _BF16),
        jax.ShapeDtypeStruct((DIFF_HEADS, nvb, DIFF_V_ROWS, V_BLOCK), _BF16),
    )
    out_specs = (
        pl.BlockSpec((FOX_HEADS, QK_FEATS, rows), lambda t: (0, 0, t)),
        pl.BlockSpec((FOX_HEADS, rows, QK_FEATS), lambda t: (0, t, 0)),
        pl.BlockSpec((FOX_HEADS, vb_per_t, FOX_V_ROWS, V_BLOCK), lambda t: (0, t, 0, 0)),
        pl.BlockSpec((DIFF_MAPS, QK_FEATS, rows), lambda t: (0, 0, t)),
        pl.BlockSpec((DIFF_MAPS, rows, QK_FEATS), lambda t: (0, t, 0)),
        pl.BlockSpec((DIFF_HEADS, vb_per_t, DIFF_V_ROWS, V_BLOCK), lambda t: (0, t, 0, 0)),
    )
    return pl.pallas_call(
        _proj_kernel,
        grid=(n_t,),
        in_specs=[
            pl.BlockSpec((rows, D_MODEL), lambda t: (t, 0)),
            pl.BlockSpec((1, D_MODEL), const2),
            pl.BlockSpec((n_feat, D_MODEL), const2),
            pl.BlockSpec((FOX_HEADS, 1), const2),
            pl.BlockSpec((DIFF_MAPS, 1), const2),
            pl.BlockSpec(gq.shape, const2),
            pl.BlockSpec(gk.shape, const2),
        ],
        out_specs=out_specs,
        out_shape=out_shapes,
        scratch_shapes=[pltpu.VMEM((FOX_HEADS, 128), _F32)],
        compiler_params=pltpu.CompilerParams(
            dimension_semantics=("arbitrary",), vmem_limit_bytes=VMEM_LIMIT_BYTES),
        name="proj",
    )(x2, g_pre_mix.reshape(1, D_MODEL), w_t, b_forget.reshape(FOX_HEADS, 1), slope_col, gq, gk)


def _flash_chains(chains, k_ref, v_ref, qi, scratch):
    blk = ATT_BLOCK
    n_full = qi
    n_chunk = blk // ROW_CHUNK
    n_tile = blk // PV_TILE
    chunks_per_tile = PV_TILE // ROW_CHUNK
    n_map = len(chains)
    s_refs = [scratch[4 * n] for n in range(n_map)]
    m_refs = [scratch[4 * n + 1] for n in range(n_map)]
    acc_refs = [scratch[4 * n + 2] for n in range(n_map)]
    bmax_refs = [scratch[4 * n + 3] for n in range(n_map)]

    def chunk_max(s_c):
        return jnp.max(s_c.reshape(ROW_CHUNK // 8, 8, s_c.shape[-1]), axis=0)

    def park_chunk(n, j, c):
        q_t, k_idx, _ = chains[n]
        start = pl.multiple_of(j * blk + c * ROW_CHUNK, ROW_CHUNK)
        s_c = jnp.dot(k_ref[k_idx, pl.ds(start, ROW_CHUNK), :], q_t, preferred_element_type=_F32)
        s_refs[n][c * ROW_CHUNK:(c + 1) * ROW_CHUNK, :] = s_c
        return chunk_max(s_c)

    def diag_scores(n, c):
        lo = c * ROW_CHUNK
        s_c = s_refs[n][lo:lo + ROW_CHUNK, lo:]
        row = lax.broadcasted_iota(jnp.int32, s_c.shape, 0)
        col = lax.broadcasted_iota(jnp.int32, s_c.shape, 1)
        return jnp.where(row <= col, s_c, MASK_VALUE)

    def step(i, diagonal=False):
        m_new, alpha = [], []
        for n in range(n_map):
            bmax = bmax_refs[n][...]
            if diagonal:
                parts = [chunk_max(diag_scores(n, c)) for c in range(n_chunk)]
                bmax = jnp.concatenate(
                    [functools.reduce(jnp.maximum,
                                      [parts[c][:, (b - c) * ROW_CHUNK:(b - c + 1) * ROW_CHUNK]
                                       for c in range(b + 1)])
                     for b in range(n_chunk)], axis=1)
            m = m_refs[n][...]
            m_new.append(jnp.maximum(m, jnp.max(bmax, axis=0, keepdims=True)))
            alpha.append(jnp.exp2(m - m_new[n]))
        pv_sum = [[None] * n_tile for _ in range(n_map)]
        next_max = [None] * n_map
        p_parts = [[] for _ in range(n_map)]
        for c in range(n_chunk):
            lo = c * ROW_CHUNK
            for n in range(n_map):
                if diagonal:
                    p_c = jnp.exp2((diag_scores(n, c) - m_new[n][:, lo:]).astype(_BF16))
                    if lo:
                        p_c = jnp.concatenate([jnp.zeros((ROW_CHUNK, lo), _BF16), p_c], axis=1)
                else:
                    p_c = jnp.exp2((s_refs[n][lo:lo + ROW_CHUNK, :] - m_new[n]).astype(_BF16))
                p_parts[n].append(p_c)
                if not diagonal:
                    cm = park_chunk(n, i + 1, c)
                    next_max[n] = cm if next_max[n] is None else jnp.maximum(next_max[n], cm)
                if len(p_parts[n]) == chunks_per_tile:
                    kt = lo // PV_TILE
                    p_tile = jnp.concatenate(p_parts[n], axis=0)
                    p_parts[n] = []
                    k0 = kt * PV_TILE
                    v_tile = v_ref[chains[n][2], i * (blk // V_BLOCK) + k0 // V_BLOCK, :,
                                   k0 % V_BLOCK:k0 % V_BLOCK + PV_TILE]
                    for nt in range(kt if diagonal else 0, n_tile):
                        piece = jnp.dot(v_tile, p_tile[:, nt * PV_TILE:(nt + 1) * PV_TILE],
                                        preferred_element_type=_F32)
                        pv_sum[n][nt] = piece if pv_sum[n][nt] is None else pv_sum[n][nt] + piece
        for n in range(n_map):
            acc_refs[n][...] = alpha[n] * acc_refs[n][...] + jnp.concatenate(pv_sum[n], axis=1)
            m_refs[n][...] = m_new[n]
            if not diagonal:
                bmax_refs[n][...] = next_max[n]

    for n in range(n_map):
        bmax_refs[n][...] = functools.reduce(
            jnp.maximum, [park_chunk(n, 0, c) for c in range(n_chunk)])
        m_refs[n][...] = jnp.full(m_refs[n].shape, MASK_VALUE, _F32)
        acc_refs[n][...] = jnp.zeros_like(acc_refs[n])

    def trip(i, _):
        step(i)
        return 0

    lax.fori_loop(0, n_full, trip, 0)
    step(n_full, diagonal=True)
    return [acc_refs[n][...] for n in range(n_map)]


def _fox_kernel(q_ref, k_ref, v_ref, o_ref, *scratch):
    n_heads = q_ref.shape[0]
    accs = _flash_chains([(q_ref[hh], hh, hh) for hh in range(n_heads)], k_ref, v_ref,
                         pl.program_id(1), scratch)
    outs = [acc[0:HEAD_DIM] / acc[HEAD_DIM:HEAD_DIM + 1] for acc in accs]
    o_t = jnp.concatenate(outs, axis=0)
    o_ref[...] = o_t.T.astype(o_ref.dtype)


def _diff_kernel(q_ref, k_ref, v_ref, lam_ref, g_ref, o_ref, *scratch):
    n_maps = q_ref.shape[0]
    accs = _flash_chains([(q_ref[n], n, n // 2) for n in range(n_maps)], k_ref, v_ref,
                         pl.program_id(1), scratch)
    outs = [acc[0:DIFF_V_DIM] / acc[DIFF_V_DIM:DIFF_V_DIM + 1] for acc in accs]
    lp = lam_ref[...]
    lam = (jnp.exp(jnp.sum(lp[0:1] * lp[1:2], axis=1, keepdims=True))
           - jnp.exp(jnp.sum(lp[2:3] * lp[3:4], axis=1, keepdims=True)) + LAM_INIT)
    heads = []
    for hd in range(n_maps // 2):
        o_t = outs[2 * hd] - lam * outs[2 * hd + 1]
        heads.append(((o_t * _rms_scale(o_t, 0)) * g_ref[...]) * (1.0 - LAM_INIT))
    o_ref[...] = jnp.concatenate(heads, axis=0).T.astype(o_ref.dtype)


def _attention(q_t, k, v_t, kernel_fn, maps_per_step, v_per_step, extra=(), name=None):
    n_maps, _, seq = q_t.shape
    nvb, v_rows = v_t.shape[1], v_t.shape[2]
    n_groups = n_maps // maps_per_step
    out_cols = maps_per_step * HEAD_DIM
    extra_specs = [pl.BlockSpec(e.shape, lambda g, qi: (0, 0)) for e in extra]
    return pl.pallas_call(
        kernel_fn,
        grid=(n_groups, seq // ATT_BLOCK),
        in_specs=[
            pl.BlockSpec((maps_per_step, QK_FEATS, ATT_BLOCK), lambda g, qi: (g, 0, qi)),
            pl.BlockSpec((maps_per_step, seq, QK_FEATS), lambda g, qi: (g, 0, 0),
                         pipeline_mode=pl.Buffered(1)),
            pl.BlockSpec((v_per_step, nvb, v_rows, V_BLOCK), lambda g, qi: (g, 0, 0, 0),
                         pipeline_mode=pl.Buffered(1)),
        ] + extra_specs,
        out_specs=pl.BlockSpec((ATT_BLOCK, out_cols), lambda g, qi: (qi, g)),
        out_shape=jax.ShapeDtypeStruct((seq, n_groups * out_cols), _BF16),
        scratch_shapes=maps_per_step * [
            pltpu.VMEM((ATT_BLOCK, ATT_BLOCK), _F32),
            pltpu.VMEM((1, ATT_BLOCK), _F32),
            pltpu.VMEM((v_rows, ATT_BLOCK), _F32),
            pltpu.VMEM((8, ATT_BLOCK), _F32)],
        compiler_params=pltpu.CompilerParams(
            dimension_semantics=("arbitrary", "arbitrary"), vmem_limit_bytes=VMEM_LIMIT_BYTES),
        name=name,
    )(q_t, k, v_t, *extra)


def _out_kernel(x_ref, fox_ref, diff_ref, wo_ref, wu_ref, wd_ref, g_ref, o_ref):
    g = g_ref[...]
    mix = (jnp.dot(fox_ref[...], wo_ref[0:FOX_WIDTH, :], preferred_element_type=_F32)
           + jnp.dot(diff_ref[...], wo_ref[FOX_WIDTH:, :], preferred_element_type=_F32))
    x1 = x_ref[...] + (mix * _rms_scale(mix, -1)) * g[0:1]
    h = ((x1 * _rms_scale(x1, -1)) * g[1:2]).astype(_BF16)
    d = jnp.zeros_like(x1)
    for f in range(D_FF // FF_CHUNK):
        u = jnp.dot(h, wu_ref[:, f * FF_CHUNK:(f + 1) * FF_CHUNK], preferred_element_type=_F32)
        u = jnp.square(jnp.maximum(u, 0.0)).astype(_BF16)
        d = d + jnp.dot(u, wd_ref[f * FF_CHUNK:(f + 1) * FF_CHUNK, :], preferred_element_type=_F32)
    o_ref[...] = x1 + (d * _rms_scale(d, -1)) * g[2:3]


def _output_stage(x2, fox, diff, w_out, w_up, w_down, gains):
    seq = x2.shape[0]
    rows = OUT_ROWS
    const2 = lambda t: (0, 0)
    resident = functools.partial(pl.BlockSpec, index_map=const2, pipeline_mode=pl.Buffered(1))
    return pl.pallas_call(
        _out_kernel,
        grid=(seq // rows,),
        in_specs=[
            pl.BlockSpec((rows, D_MODEL), lambda t: (t, 0)),
            pl.BlockSpec((rows, FOX_WIDTH), lambda t: (t, 0)),
            pl.BlockSpec((rows, DIFF_WIDTH), lambda t: (t, 0)),
            resident((D_MODEL, D_MODEL)),
            resident((D_MODEL, D_FF)),
            resident((D_FF, D_MODEL)),
            pl.BlockSpec((3, D_MODEL), const2),
        ],
        out_specs=pl.BlockSpec((rows, D_MODEL), lambda t: (t, 0)),
        out_shape=jax.ShapeDtypeStruct((seq, D_MODEL), _F32),
        compiler_params=pltpu.CompilerParams(
            dimension_semantics=("arbitrary",), vmem_limit_bytes=VMEM_LIMIT_BYTES),
        name="out_mlp",
    )(x2, fox, diff, w_out, w_up, w_down, gains)


def kernel(x, g_pre_mix, w_in, b_forget, lambda_q1, lambda_k1, lambda_q2, lambda_k2,
           g_subln, w_out, g_post_mix, g_pre_mlp, w_up, w_down, g_post_mlp):
    batch, seq, _ = x.shape
    assert batch == 1 and w_in.shape[0] == 1
    assert seq % PROJ_ROWS == 0 and seq % ATT_BLOCK == 0 and seq % OUT_ROWS == 0
    assert PROJ_ROWS % V_BLOCK == 0 and ATT_BLOCK % V_BLOCK == 0
    assert V_BLOCK % PV_TILE == 0 and PV_TILE % ROW_CHUNK == 0
    x2 = x.reshape(seq, D_MODEL)
    fq_t, fk, fv_t, dq_t, dk, dv_t = _projection(x2, g_pre_mix[0], w_in[0], b_forget[0])
    fox = _attention(fq_t, fk, fv_t, _fox_kernel, ATT_MAPS, ATT_MAPS, name="fox_attn")
    lam_params = jnp.stack([lambda_q1[0], lambda_k1[0], lambda_q2[0], lambda_k2[0]]).astype(_F32)
    diff = _attention(dq_t, dk, dv_t, _diff_kernel, ATT_MAPS, ATT_MAPS // 2,
                      extra=(lam_params, g_subln[0].reshape(DIFF_V_DIM, 1)), name="diff_attn")
    gains = jnp.stack([g_post_mix[0], g_pre_mlp[0], g_post_mlp[0]])
    out = _output_stage(x2, fox, diff, w_out[0].astype(_BF16), w_up[0].astype(_BF16),
                        w_down[0].astype(_BF16), gains)
    return out.reshape(batch, seq, D_MODEL)
```

```python
import functools
import math

import numpy as np
import jax
import jax.numpy as jnp
from jax import lax
from jax.experimental import pallas as pl
from jax.experimental.pallas import tpu as pltpu

D_MODEL = 1024
HEAD_DIM = 64
FOX_HEADS = 8
FOX_WIDTH = FOX_HEADS * HEAD_DIM
DIFF_HEADS = 4
DIFF_MAPS = 2 * DIFF_HEADS
DIFF_V_DIM = 2 * HEAD_DIM
DIFF_WIDTH = DIFF_HEADS * DIFF_V_DIM
D_FF = 4 * D_MODEL
RMS_EPS = 1e-6
LAM_INIT = 0.8 - 0.6 * math.exp(-0.3 * 0)
SCORE_SCALE = HEAD_DIM ** -0.5
LOG2E = math.log2(math.e)

LANES = 128
SUBLANES = 8
PACKED_ROWS = 16
MXU_TILE = 256
VMEM_BYTES = 64 * 1024 * 1024

N_BIAS_MAPS = 8
QK_FEATS = LANES
BIAS_ROWS = PACKED_ROWS
FOX_V_ROWS = HEAD_DIM + BIAS_ROWS
DIFF_V_ROWS = DIFF_V_DIM + BIAS_ROWS
MASK_VALUE = -1e30

PROJ_ROWS = 512
ATT_BLOCK = 1024
ATT_MAPS = 4
ROW_CHUNK = 512
DIAG_CHUNK = 256
PV_TILE = MXU_TILE
V_BLOCK = 512
OUT_ROWS = 1024
FF_CHUNK = 1024
VMEM_LIMIT_BYTES = VMEM_BYTES - 8 * 1024 * 1024

_F32 = jnp.float32
_BF16 = jnp.bfloat16


def _split3(v):
    hi = v.astype(_BF16).astype(_F32)
    r = v - hi
    mid = r.astype(_BF16).astype(_F32)
    lo = (r - mid).astype(_BF16).astype(_F32)
    return hi, mid, lo


def _bias_selectors():
    n = N_BIAS_MAPS
    gq = np.zeros((n * BIAS_ROWS, 4 * n), np.float32)
    gk = np.zeros((n * BIAS_ROWS, 4 * n), np.float32)
    for h in range(n):
        for part in range(3):
            gq[h * BIAS_ROWS + part, part * n + h] = 1.0
            gq[h * BIAS_ROWS + 3 + part, 3 * n + h] = 1.0
            gk[h * BIAS_ROWS + part, 3 * n + h] = 1.0
            gk[h * BIAS_ROWS + 3 + part, part * n + h] = -1.0
    return jnp.asarray(gq, _BF16), jnp.asarray(gk, _BF16)


def _rms_scale(v, axis):
    return lax.rsqrt(jnp.mean(v * v, axis=axis, keepdims=True) + RMS_EPS)


def _proj_kernel(x_ref, g_ref, w_ref, bf_ref, slope_ref, gq_ref, gk_ref,
                 fq_ref, fk_ref, fv_ref, dq_ref, dk_ref, dv_ref, carry_ref):
    t = pl.program_id(0)
    rows = x_ref.shape[0]

    @pl.when(t == 0)
    def _():
        carry_ref[...] = jnp.zeros_like(carry_ref)

    x = x_ref[...]
    h = ((x * _rms_scale(x, -1)) * g_ref[...]).astype(_BF16)

    def proj_t(lo, hi):
        return lax.dot_general(w_ref[lo:hi, :], h, (((1,), (1,)), ((), ())),
                               preferred_element_type=_F32)

    nb = N_BIAS_MAPS
    ones_rows = jnp.ones((nb, rows), _F32)

    def bias_feats(c):
        c_hi, c_mid, c_lo = _split3(c * LOG2E)
        c3 = jnp.concatenate([c_hi, c_mid, c_lo, ones_rows], axis=0).astype(_BF16)
        fq = jnp.dot(gq_ref[...], c3, preferred_element_type=_F32)
        fk = jnp.dot(gk_ref[...], c3, preferred_element_type=_F32)
        return fq, fk

    f0 = 3 * FOX_WIDTH + 3 * DIFF_WIDTH
    fl = proj_t(f0, f0 + PACKED_ROWS)[0:FOX_HEADS]
    fl = fl + bf_ref[...]
    ls = jnp.minimum(fl, 0.0) - jnp.log1p(jnp.exp(-jnp.abs(fl)))
    l_hi, l_mid, l_lo = _split3(ls)
    stack = jnp.concatenate([l_hi, l_mid, l_lo, jnp.zeros((nb, rows), _F32)], axis=0).astype(_BF16)
    r_i = lax.broadcasted_iota(jnp.int32, (rows, rows), 0)
    c_i = lax.broadcasted_iota(jnp.int32, (rows, rows), 1)
    upper = (r_i <= c_i).astype(_BF16)
    cs = jnp.dot(stack, upper, preferred_element_type=_F32)
    c_fox = carry_ref[:, 0:1] + ((cs[0:nb] + cs[nb:2 * nb]) + cs[2 * nb:3 * nb])
    carry_ref[...] = jnp.broadcast_to(c_fox[:, rows - 1:rows], carry_ref.shape)

    pos = (lax.broadcasted_iota(jnp.int32, (nb, rows), 1) + t * rows).astype(_F32)
    c_alibi = -(slope_ref[...] * pos)

    def ones_then_zeros(n):
        return (lax.broadcasted_iota(jnp.int32, (n, rows), 0) == 0).astype(_BF16)

    zero_pad_q = jnp.zeros((QK_FEATS - HEAD_DIM - BIAS_ROWS, rows), _BF16)
    zero_pad_k = jnp.zeros((QK_FEATS - HEAD_DIM - BIAS_ROWS, rows), _F32)
    vb = fv_ref.shape[-1]

    def emit_qk(q_t, k_t, c, q_out, k_out):
        feat_q, feat_k = bias_feats(c)
        for m in range(nb):
            d0, b0 = m * HEAD_DIM, m * BIAS_ROWS
            q_out[m, 0:HEAD_DIM, :] = (q_t[d0:d0 + HEAD_DIM] * (SCORE_SCALE * LOG2E)).astype(_BF16)
            q_out[m, HEAD_DIM:HEAD_DIM + BIAS_ROWS, :] = feat_q[b0:b0 + BIAS_ROWS].astype(_BF16)
            q_out[m, HEAD_DIM + BIAS_ROWS:, :] = zero_pad_q
            k_aug = jnp.concatenate(
                [k_t[d0:d0 + HEAD_DIM], feat_k[b0:b0 + BIAS_ROWS], zero_pad_k], axis=0)
            k_out[m] = k_aug.T.astype(_BF16)

    def emit_v(v_t, n_heads, v_dim, v_out):
        v_bf = v_t.astype(_BF16)
        pad_rows = ones_then_zeros(v_out.shape[2] - v_dim)
        for hd in range(n_heads):
            for b in range(rows // vb):
                v_out[hd, b, 0:v_dim, :] = v_bf[hd * v_dim:(hd + 1) * v_dim, b * vb:(b + 1) * vb]
                v_out[hd, b, v_dim:, :] = pad_rows[:, b * vb:(b + 1) * vb]

    o = 0
    fq_t = proj_t(o, o + FOX_WIDTH); o += FOX_WIDTH
    fk_t = proj_t(o, o + FOX_WIDTH); o += FOX_WIDTH
    emit_qk(fq_t, fk_t, c_fox, fq_ref, fk_ref)
    fv_t = proj_t(o, o + FOX_WIDTH); o += FOX_WIDTH
    emit_v(fv_t, FOX_HEADS, HEAD_DIM, fv_ref)
    dq_t = proj_t(o, o + DIFF_WIDTH); o += DIFF_WIDTH
    dk_t = proj_t(o, o + DIFF_WIDTH); o += DIFF_WIDTH
    emit_qk(dq_t, dk_t, c_alibi, dq_ref, dk_ref)
    dv_t = proj_t(o, o + DIFF_WIDTH); o += DIFF_WIDTH
    emit_v(dv_t, DIFF_HEADS, DIFF_V_DIM, dv_ref)


def _projection(x2, g_pre_mix, w_in, b_forget):
    seq = x2.shape[0]
    rows = PROJ_ROWS
    n_t = seq // rows
    nvb = seq // V_BLOCK
    vb_per_t = rows // V_BLOCK
    f0 = 3 * FOX_WIDTH
    w_main = jnp.concatenate([w_in[:, :f0], w_in[:, f0 + FOX_HEADS:]], axis=1)
    w_f = jnp.pad(w_in[:, f0:f0 + FOX_HEADS], ((0, 0), (0, PACKED_ROWS - FOX_HEADS)))
    w_t = jnp.concatenate([w_main, w_f], axis=1).T.astype(_BF16)
    n_feat = w_t.shape[0]
    slopes = 2.0 ** (-8.0 * np.arange(1, DIFF_HEADS + 1) / DIFF_HEADS)
    slope_col = jnp.asarray(np.repeat(slopes, 2).reshape(DIFF_MAPS, 1), _F32)
    gq, gk = _bias_selectors()

    const2 = lambda t: (0, 0)
    out_shapes = (
        jax.ShapeDtypeStruct((FOX_HEADS, QK_FEATS, seq), _BF16),
        jax.ShapeDtypeStruct((FOX_HEADS, seq, QK_FEATS), _BF16),
        jax.ShapeDtypeStruct((FOX_HEADS, nvb, FOX_V_ROWS, V_BLOCK), _BF16),
        jax.ShapeDtypeStruct((DIFF_MAPS, QK_FEATS, seq), _BF16),
        jax.ShapeDtypeStruct((DIFF_MAPS, seq, QK_FEATS), _BF16),
        jax.ShapeDtypeStruct((DIFF_HEADS, nvb, DIFF_V_ROWS, V_BLOCK), _BF16),
    )
    out_specs = (
        pl.BlockSpec((FOX_HEADS, QK_FEATS, rows), lambda t: (0, 0, t)),
        pl.BlockSpec((FOX_HEADS, rows, QK_FEATS), lambda t: (0, t, 0)),
        pl.BlockSpec((FOX_HEADS, vb_per_t, FOX_V_ROWS, V_BLOCK), lambda t: (0, t, 0, 0)),
        pl.BlockSpec((DIFF_MAPS, QK_FEATS, rows), lambda t: (0, 0, t)),
        pl.BlockSpec((DIFF_MAPS, rows, QK_FEATS), lambda t: (0, t, 0)),
        pl.BlockSpec((DIFF_HEADS, vb_per_t, DIFF_V_ROWS, V_BLOCK), lambda t: (0, t, 0, 0)),
    )
    return pl.pallas_call(
        _proj_kernel,
        grid=(n_t,),
        in_specs=[
            pl.BlockSpec((rows, D_MODEL), lambda t: (t, 0)),
            pl.BlockSpec((1, D_MODEL), const2),
            pl.BlockSpec((n_feat, D_MODEL), const2),
            pl.BlockSpec((FOX_HEADS, 1), const2),
            pl.BlockSpec((DIFF_MAPS, 1), const2),
            pl.BlockSpec(gq.shape, const2),
            pl.BlockSpec(gk.shape, const2),
        ],
        out_specs=out_specs,
        out_shape=out_shapes,
        scratch_shapes=[pltpu.VMEM((FOX_HEADS, LANES), _F32)],
        compiler_params=pltpu.CompilerParams(
            dimension_semantics=("arbitrary",), vmem_limit_bytes=VMEM_LIMIT_BYTES),
        name="proj",
    )(x2, g_pre_mix.reshape(1, D_MODEL), w_t, b_forget.reshape(FOX_HEADS, 1), slope_col, gq, gk)


def _flash_chains(chains, k_ref, v_ref, qi, scratch):
    blk = ATT_BLOCK
    n_full = qi
    n_chunk = blk // ROW_CHUNK
    n_tile = blk // PV_TILE
    n_map = len(chains)
    s_refs = [scratch[4 * n] for n in range(n_map)]
    m_refs = [scratch[4 * n + 1] for n in range(n_map)]
    acc_refs = [scratch[4 * n + 2] for n in range(n_map)]
    bmax_refs = [scratch[4 * n + 3] for n in range(n_map)]

    def chunk_max(s_c):
        return jnp.max(s_c.reshape(s_c.shape[0] // SUBLANES, SUBLANES, s_c.shape[-1]), axis=0)

    def park_chunk(n, j, c):
        q_t, k_idx, _ = chains[n]
        start = pl.multiple_of(j * blk + c * ROW_CHUNK, ROW_CHUNK)
        s_c = jnp.dot(k_ref[k_idx, pl.ds(start, ROW_CHUNK), :], q_t, preferred_element_type=_F32)
        s_refs[n][c * ROW_CHUNK:(c + 1) * ROW_CHUNK, :] = s_c
        return chunk_max(s_c)

    def diag_scores(n, c):
        lo = c * DIAG_CHUNK
        s_c = s_refs[n][lo:lo + DIAG_CHUNK, lo:]
        row = lax.broadcasted_iota(jnp.int32, s_c.shape, 0)
        col = lax.broadcasted_iota(jnp.int32, s_c.shape, 1)
        return jnp.where(row <= col, s_c, MASK_VALUE)

    def step(i, diagonal=False):
        rc = DIAG_CHUNK if diagonal else ROW_CHUNK
        m_new, alpha = [], []
        for n in range(n_map):
            bmax = bmax_refs[n][...]
            if diagonal:
                parts = [chunk_max(diag_scores(n, c)) for c in range(blk // rc)]
                bmax = jnp.concatenate(
                    [functools.reduce(jnp.maximum,
                                      [parts[c][:, (b - c) * rc:(b - c + 1) * rc]
                                       for c in range(b + 1)])
                     for b in range(blk // rc)], axis=1)
            m = m_refs[n][...]
            m_new.append(jnp.maximum(m, jnp.max(bmax, axis=0, keepdims=True)))
            alpha.append(jnp.exp2(m - m_new[n]))
        pv_sum = [[None] * n_tile for _ in range(n_map)]
        next_max = [None] * n_map
        p_parts = [[] for _ in range(n_map)]
        for c in range(blk // rc):
            lo = c * rc
            for n in range(n_map):
                if diagonal:
                    p_c = jnp.exp2((diag_scores(n, c) - m_new[n][:, lo:]).astype(_BF16))
                    if lo:
                        p_c = jnp.concatenate([jnp.zeros((rc, lo), _BF16), p_c], axis=1)
                else:
                    p_c = jnp.exp2((s_refs[n][lo:lo + rc, :] - m_new[n]).astype(_BF16))
                p_parts[n].append(p_c)
                if not diagonal:
                    cm = park_chunk(n, i + 1, c)
                    next_max[n] = cm if next_max[n] is None else jnp.maximum(next_max[n], cm)
                if (lo + rc) % PV_TILE:
                    continue
                p_rows = jnp.concatenate(p_parts[n], axis=0)
                p_parts[n] = []
                k_first = lo + rc - p_rows.shape[0]
                for k0 in range(k_first, lo + rc, PV_TILE):
                    kt = k0 // PV_TILE
                    p_tile = p_rows[k0 - k_first:k0 - k_first + PV_TILE]
                    v_tile = v_ref[chains[n][2], i * (blk // V_BLOCK) + k0 // V_BLOCK, :,
                                   k0 % V_BLOCK:k0 % V_BLOCK + PV_TILE]
                    for nt in range(kt if diagonal else 0, n_tile):
                        piece = jnp.dot(v_tile, p_tile[:, nt * PV_TILE:(nt + 1) * PV_TILE],
                                        preferred_element_type=_F32)
                        pv_sum[n][nt] = piece if pv_sum[n][nt] is None else pv_sum[n][nt] + piece
        for n in range(n_map):
            acc_refs[n][...] = alpha[n] * acc_refs[n][...] + jnp.concatenate(pv_sum[n], axis=1)
            m_refs[n][...] = m_new[n]
            if not diagonal:
                bmax_refs[n][...] = next_max[n]

    for n in range(n_map):
        bmax_refs[n][...] = functools.reduce(
            jnp.maximum, [park_chunk(n, 0, c) for c in range(n_chunk)])
        m_refs[n][...] = jnp.full(m_refs[n].shape, MASK_VALUE, _F32)
        acc_refs[n][...] = jnp.zeros_like(acc_refs[n])

    def trip(i, _):
        step(i)
        return 0

    lax.fori_loop(0, n_full, trip, 0)
    step(n_full, diagonal=True)
    return [acc_refs[n][...] for n in range(n_map)]


def _fox_kernel(q_ref, k_ref, v_ref, o_ref, *scratch):
    n_heads = q_ref.shape[0]
    accs = _flash_chains([(q_ref[hh], hh, hh) for hh in range(n_heads)], k_ref, v_ref,
                         pl.program_id(1), scratch)
    outs = [acc[0:HEAD_DIM] / acc[HEAD_DIM:HEAD_DIM + 1] for acc in accs]
    o_t = jnp.concatenate(outs, axis=0)
    o_ref[...] = o_t.T.astype(o_ref.dtype)


def _diff_kernel(q_ref, k_ref, v_ref, lam_ref, g_ref, o_ref, *scratch):
    n_maps = q_ref.shape[0]
    accs = _flash_chains([(q_ref[n], n, n // 2) for n in range(n_maps)], k_ref, v_ref,
                         pl.program_id(1), scratch)
    outs = [acc[0:DIFF_V_DIM] / acc[DIFF_V_DIM:DIFF_V_DIM + 1] for acc in accs]
    lp = lam_ref[...]
    lam = (jnp.exp(jnp.sum(lp[0:1] * lp[1:2], axis=1, keepdims=True))
           - jnp.exp(jnp.sum(lp[2:3] * lp[3:4], axis=1, keepdims=True)) + LAM_INIT)
    heads = []
    for hd in range(n_maps // 2):
        o_t = outs[2 * hd] - lam * outs[2 * hd + 1]
        heads.append(((o_t * _rms_scale(o_t, 0)) * g_ref[...]) * (1.0 - LAM_INIT))
    o_ref[...] = jnp.concatenate(heads, axis=0).T.astype(o_ref.dtype)


def _attention(q_t, k, v_t, kernel_fn, maps_per_step, v_per_step, extra=(), name=None):
    n_maps, _, seq = q_t.shape
    nvb, v_rows = v_t.shape[1], v_t.shape[2]
    n_groups = n_maps // maps_per_step
    out_cols = maps_per_step * HEAD_DIM
    extra_specs = [pl.BlockSpec(e.shape, lambda g, qi: (0, 0)) for e in extra]
    return pl.pallas_call(
        kernel_fn,
        grid=(n_groups, seq // ATT_BLOCK),
        in_specs=[
            pl.BlockSpec((maps_per_step, QK_FEATS, ATT_BLOCK), lambda g, qi: (g, 0, qi)),
            pl.BlockSpec((maps_per_step, seq, QK_FEATS), lambda g, qi: (g, 0, 0),
                         pipeline_mode=pl.Buffered(1)),
            pl.BlockSpec((v_per_step, nvb, v_rows, V_BLOCK), lambda g, qi: (g, 0, 0, 0),
                         pipeline_mode=pl.Buffered(1)),
        ] + extra_specs,
        out_specs=pl.BlockSpec((ATT_BLOCK, out_cols), lambda g, qi: (qi, g)),
        out_shape=jax.ShapeDtypeStruct((seq, n_groups * out_cols), _BF16),
        scratch_shapes=maps_per_step * [
            pltpu.VMEM((ATT_BLOCK, ATT_BLOCK), _F32),
            pltpu.VMEM((1, ATT_BLOCK), _F32),
            pltpu.VMEM((v_rows, ATT_BLOCK), _F32),
            pltpu.VMEM((SUBLANES, ATT_BLOCK), _F32)],
        compiler_params=pltpu.CompilerParams(
            dimension_semantics=("arbitrary", "arbitrary"), vmem_limit_bytes=VMEM_LIMIT_BYTES),
        name=name,
    )(q_t, k, v_t, *extra)


def _out_kernel(x_ref, fox_ref, diff_ref, wo_ref, wu_ref, wd_ref, g_ref, o_ref):
    g = g_ref[...]
    mix = (jnp.dot(fox_ref[...], wo_ref[0:FOX_WIDTH, :], preferred_element_type=_F32)
           + jnp.dot(diff_ref[...], wo_ref[FOX_WIDTH:, :], preferred_element_type=_F32))
    x1 = x_ref[...] + (mix * _rms_scale(mix, -1)) * g[0:1]
    h = ((x1 * _rms_scale(x1, -1)) * g[1:2]).astype(_BF16)
    d = jnp.zeros_like(x1)
    for f in range(D_FF // FF_CHUNK):
        u = jnp.dot(h, wu_ref[:, f * FF_CHUNK:(f + 1) * FF_CHUNK], preferred_element_type=_F32)
        u = jnp.square(jnp.maximum(u, 0.0)).astype(_BF16)
        d = d + jnp.dot(u, wd_ref[f * FF_CHUNK:(f + 1) * FF_CHUNK, :], preferred_element_type=_F32)
    o_ref[...] = x1 + (d * _rms_scale(d, -1)) * g[2:3]


def _output_stage(x2, fox, diff, w_out, w_up, w_down, gains):
    seq = x2.shape[0]
    rows = OUT_ROWS
    const2 = lambda t: (0, 0)
    resident = functools.partial(pl.BlockSpec, index_map=const2, pipeline_mode=pl.Buffered(1))
    return pl.pallas_call(
        _out_kernel,
        grid=(seq // rows,),
        in_specs=[
            pl.BlockSpec((rows, D_MODEL), lambda t: (t, 0)),
            pl.BlockSpec((rows, FOX_WIDTH), lambda t: (t, 0)),
            pl.BlockSpec((rows, DIFF_WIDTH), lambda t: (t, 0)),
            resident((D_MODEL, D_MODEL)),
            resident((D_MODEL, D_FF)),
            resident((D_FF, D_MODEL)),
            pl.BlockSpec((3, D_MODEL), const2),
        ],
        out_specs=pl.BlockSpec((rows, D_MODEL), lambda t: (t, 0)),
        out_shape=jax.ShapeDtypeStruct((seq, D_MODEL), _F32),
        compiler_params=pltpu.CompilerParams(
            dimension_semantics=("arbitrary",), vmem_limit_bytes=VMEM_LIMIT_BYTES),
        name="out_mlp",
    )(x2, fox, diff, w_out, w_up, w_down, gains)


def kernel(x, g_pre_mix, w_in, b_forget, lambda_q1, lambda_k1, lambda_q2, lambda_k2,
           g_subln, w_out, g_post_mix, g_pre_mlp, w_up, w_down, g_post_mlp):
    batch, seq, _ = x.shape
    assert batch == 1 and w_in.shape[0] == 1
    assert FOX_HEADS == DIFF_MAPS == N_BIAS_MAPS
    assert seq % PROJ_ROWS == 0 and seq % ATT_BLOCK == 0 and seq % OUT_ROWS == 0
    assert PROJ_ROWS % V_BLOCK == 0 and ATT_BLOCK % V_BLOCK == 0
    assert V_BLOCK % PV_TILE == 0 and (PV_TILE % ROW_CHUNK == 0 or ROW_CHUNK % PV_TILE == 0)
    x2 = x.reshape(seq, D_MODEL)
    fq_t, fk, fv_t, dq_t, dk, dv_t = _projection(x2, g_pre_mix[0], w_in[0], b_forget[0])
    fox = _attention(fq_t, fk, fv_t, _fox_kernel, ATT_MAPS, ATT_MAPS, name="fox_attn")
    lam_params = jnp.stack([lambda_q1[0], lambda_k1[0], lambda_q2[0], lambda_k2[0]]).astype(_F32)
    diff = _attention(dq_t, dk, dv_t, _diff_kernel, ATT_MAPS, ATT_MAPS // 2,
                      extra=(lam_params, g_subln[0].reshape(DIFF_V_DIM, 1)), name="diff_attn")
    gains = jnp.stack([g_post_mix[0], g_pre_mlp[0], g_post_mlp[0]])
    out = _output_stage(x2, fox, diff, w_out[0].astype(_BF16), w_up[0].astype(_BF16),
                        w_down[0].astype(_BF16), gains)
    return out.reshape(batch, seq, D_MODEL)
```

```python
import functools
import math

import numpy as np
import jax
import jax.numpy as jnp
from jax import lax
from jax.experimental import pallas as pl
from jax.experimental.pallas import tpu as pltpu

D_MODEL = 1024
HEAD_DIM = 64
FOX_HEADS = 8
FOX_WIDTH = FOX_HEADS * HEAD_DIM
DIFF_HEADS = 4
DIFF_MAPS = 2 * DIFF_HEADS
DIFF_V_DIM = 2 * HEAD_DIM
DIFF_WIDTH = DIFF_HEADS * DIFF_V_DIM
D_FF = 4 * D_MODEL
RMS_EPS = 1e-6
LAM_INIT = 0.8 - 0.6 * math.exp(-0.3 * 0)
SCORE_SCALE = HEAD_DIM ** -0.5
LOG2E = math.log2(math.e)

LANES = 128
SUBLANES = 8
PACKED_ROWS = 16
MXU_TILE = 256
VMEM_BYTES = 64 * 1024 * 1024

N_BIAS_MAPS = 8
QK_FEATS = LANES
BIAS_ROWS = PACKED_ROWS
FOX_V_ROWS = HEAD_DIM + BIAS_ROWS
DIFF_V_ROWS = DIFF_V_DIM + BIAS_ROWS
MASK_VALUE = -1e30

PROJ_ROWS = 512
ATT_BLOCK = 1024
ATT_MAPS = 4
ATT_LOCKSTEP = 2
ROW_CHUNK = 512
DIAG_CHUNK = 256
PV_TILE = MXU_TILE
V_BLOCK = 512
OUT_ROWS = 1024
FF_CHUNK = 1024
VMEM_LIMIT_BYTES = VMEM_BYTES - 8 * 1024 * 1024

_F32 = jnp.float32
_BF16 = jnp.bfloat16


def _split3(v):
    hi = v.astype(_BF16).astype(_F32)
    r = v - hi
    mid = r.astype(_BF16).astype(_F32)
    lo = (r - mid).astype(_BF16).astype(_F32)
    return hi, mid, lo


def _bias_selectors():
    n = N_BIAS_MAPS
    gq = np.zeros((n * BIAS_ROWS, 4 * n), np.float32)
    gk = np.zeros((n * BIAS_ROWS, 4 * n), np.float32)
    for h in range(n):
        for part in range(3):
            gq[h * BIAS_ROWS + part, part * n + h] = 1.0
            gq[h * BIAS_ROWS + 3 + part, 3 * n + h] = 1.0
            gk[h * BIAS_ROWS + part, 3 * n + h] = 1.0
            gk[h * BIAS_ROWS + 3 + part, part * n + h] = -1.0
    return jnp.asarray(gq, _BF16), jnp.asarray(gk, _BF16)


def _rms_scale(v, axis):
    return lax.rsqrt(jnp.mean(v * v, axis=axis, keepdims=True) + RMS_EPS)


def _proj_kernel(x_ref, g_ref, w_ref, bf_ref, slope_ref, gq_ref, gk_ref,
                 fq_ref, fk_ref, fv_ref, dq_ref, dk_ref, dv_ref, carry_ref):
    t = pl.program_id(0)
    rows = x_ref.shape[0]

    @pl.when(t == 0)
    def _():
        carry_ref[...] = jnp.zeros_like(carry_ref)

    x = x_ref[...]
    h = ((x * _rms_scale(x, -1)) * g_ref[...]).astype(_BF16)

    def proj_t(lo, hi):
        return lax.dot_general(w_ref[lo:hi, :], h, (((1,), (1,)), ((), ())),
                               preferred_element_type=_F32)

    nb = N_BIAS_MAPS
    ones_rows = jnp.ones((nb, rows), _F32)

    def bias_feats(c):
        c_hi, c_mid, c_lo = _split3(c * LOG2E)
        c3 = jnp.concatenate([c_hi, c_mid, c_lo, ones_rows], axis=0).astype(_BF16)
        fq = jnp.dot(gq_ref[...], c3, preferred_element_type=_F32)
        fk = jnp.dot(gk_ref[...], c3, preferred_element_type=_F32)
        return fq, fk

    f0 = 3 * FOX_WIDTH + 3 * DIFF_WIDTH
    fl = proj_t(f0, f0 + PACKED_ROWS)[0:FOX_HEADS]
    fl = fl + bf_ref[...]
    ls = jnp.minimum(fl, 0.0) - jnp.log1p(jnp.exp(-jnp.abs(fl)))
    l_hi, l_mid, l_lo = _split3(ls)
    stack = jnp.concatenate([l_hi, l_mid, l_lo, jnp.zeros((nb, rows), _F32)], axis=0).astype(_BF16)
    r_i = lax.broadcasted_iota(jnp.int32, (rows, rows), 0)
    c_i = lax.broadcasted_iota(jnp.int32, (rows, rows), 1)
    upper = (r_i <= c_i).astype(_BF16)
    cs = jnp.dot(stack, upper, preferred_element_type=_F32)
    c_fox = carry_ref[:, 0:1] + ((cs[0:nb] + cs[nb:2 * nb]) + cs[2 * nb:3 * nb])
    carry_ref[...] = jnp.broadcast_to(c_fox[:, rows - 1:rows], carry_ref.shape)

    pos = (lax.broadcasted_iota(jnp.int32, (nb, rows), 1) + t * rows).astype(_F32)
    c_alibi = -(slope_ref[...] * pos)

    def ones_then_zeros(n):
        return (lax.broadcasted_iota(jnp.int32, (n, rows), 0) == 0).astype(_BF16)

    zero_pad_q = jnp.zeros((QK_FEATS - HEAD_DIM - BIAS_ROWS, rows), _BF16)
    zero_pad_k = jnp.zeros((QK_FEATS - HEAD_DIM - BIAS_ROWS, rows), _F32)
    vb = fv_ref.shape[-1]

    def emit_qk(q_t, k_t, c, q_out, k_out):
        feat_q, feat_k = bias_feats(c)
        for m in range(nb):
            d0, b0 = m * HEAD_DIM, m * BIAS_ROWS
            q_out[m, 0:HEAD_DIM, :] = (q_t[d0:d0 + HEAD_DIM] * (SCORE_SCALE * LOG2E)).astype(_BF16)
            q_out[m, HEAD_DIM:HEAD_DIM + BIAS_ROWS, :] = feat_q[b0:b0 + BIAS_ROWS].astype(_BF16)
            q_out[m, HEAD_DIM + BIAS_ROWS:, :] = zero_pad_q
            k_aug = jnp.concatenate(
                [k_t[d0:d0 + HEAD_DIM], feat_k[b0:b0 + BIAS_ROWS], zero_pad_k], axis=0)
            k_out[m] = k_aug.T.astype(_BF16)

    def emit_v(v_t, n_heads, v_dim, v_out):
        v_bf = v_t.astype(_BF16)
        pad_rows = ones_then_zeros(v_out.shape[2] - v_dim)
        for hd in range(n_heads):
            for b in range(rows // vb):
                v_out[hd, b, 0:v_dim, :] = v_bf[hd * v_dim:(hd + 1) * v_dim, b * vb:(b + 1) * vb]
                v_out[hd, b, v_dim:, :] = pad_rows[:, b * vb:(b + 1) * vb]

    o = 0
    fq_t = proj_t(o, o + FOX_WIDTH); o += FOX_WIDTH
    fk_t = proj_t(o, o + FOX_WIDTH); o += FOX_WIDTH
    emit_qk(fq_t, fk_t, c_fox, fq_ref, fk_ref)
    fv_t = proj_t(o, o + FOX_WIDTH); o += FOX_WIDTH
    emit_v(fv_t, FOX_HEADS, HEAD_DIM, fv_ref)
    dq_t = proj_t(o, o + DIFF_WIDTH); o += DIFF_WIDTH
    dk_t = proj_t(o, o + DIFF_WIDTH); o += DIFF_WIDTH
    emit_qk(dq_t, dk_t, c_alibi, dq_ref, dk_ref)
    dv_t = proj_t(o, o + DIFF_WIDTH); o += DIFF_WIDTH
    emit_v(dv_t, DIFF_HEADS, DIFF_V_DIM, dv_ref)


def _projection(x2, g_pre_mix, w_in, b_forget):
    seq = x2.shape[0]
    rows = PROJ_ROWS
    n_t = seq // rows
    nvb = seq // V_BLOCK
    vb_per_t = rows // V_BLOCK
    f0 = 3 * FOX_WIDTH
    w_main = jnp.concatenate([w_in[:, :f0], w_in[:, f0 + FOX_HEADS:]], axis=1)
    w_f = jnp.pad(w_in[:, f0:f0 + FOX_HEADS], ((0, 0), (0, PACKED_ROWS - FOX_HEADS)))
    w_t = jnp.concatenate([w_main, w_f], axis=1).T.astype(_BF16)
    n_feat = w_t.shape[0]
    slopes = 2.0 ** (-8.0 * np.arange(1, DIFF_HEADS + 1) / DIFF_HEADS)
    slope_col = jnp.asarray(np.repeat(slopes, 2).reshape(DIFF_MAPS, 1), _F32)
    gq, gk = _bias_selectors()

    const2 = lambda t: (0, 0)
    out_shapes = (
        jax.ShapeDtypeStruct((FOX_HEADS, QK_FEATS, seq), _BF16),
        jax.ShapeDtypeStruct((FOX_HEADS, seq, QK_FEATS), _BF16),
        jax.ShapeDtypeStruct((FOX_HEADS, nvb, FOX_V_ROWS, V_BLOCK), _BF16),
        jax.ShapeDtypeStruct((DIFF_MAPS, QK_FEATS, seq), _BF16),
        jax.ShapeDtypeStruct((DIFF_MAPS, seq, QK_FEATS), _BF16),
        jax.ShapeDtypeStruct((DIFF_HEADS, nvb, DIFF_V_ROWS, V_BLOCK), _BF16),
    )
    out_specs = (
        pl.BlockSpec((FOX_HEADS, QK_FEATS, rows), lambda t: (0, 0, t)),
        pl.BlockSpec((FOX_HEADS, rows, QK_FEATS), lambda t: (0, t, 0)),
        pl.BlockSpec((FOX_HEADS, vb_per_t, FOX_V_ROWS, V_BLOCK), lambda t: (0, t, 0, 0)),
        pl.BlockSpec((DIFF_MAPS, QK_FEATS, rows), lambda t: (0, 0, t)),
        pl.BlockSpec((DIFF_MAPS, rows, QK_FEATS), lambda t: (0, t, 0)),
        pl.BlockSpec((DIFF_HEADS, vb_per_t, DIFF_V_ROWS, V_BLOCK), lambda t: (0, t, 0, 0)),
    )
    return pl.pallas_call(
        _proj_kernel,
        grid=(n_t,),
        in_specs=[
            pl.BlockSpec((rows, D_MODEL), lambda t: (t, 0)),
            pl.BlockSpec((1, D_MODEL), const2),
            pl.BlockSpec((n_feat, D_MODEL), const2),
            pl.BlockSpec((FOX_HEADS, 1), const2),
            pl.BlockSpec((DIFF_MAPS, 1), const2),
            pl.BlockSpec(gq.shape, const2),
            pl.BlockSpec(gk.shape, const2),
        ],
        out_specs=out_specs,
        out_shape=out_shapes,
        scratch_shapes=[pltpu.VMEM((FOX_HEADS, LANES), _F32)],
        compiler_params=pltpu.CompilerParams(
            dimension_semantics=("arbitrary",), vmem_limit_bytes=VMEM_LIMIT_BYTES),
        name="proj",
    )(x2, g_pre_mix.reshape(1, D_MODEL), w_t, b_forget.reshape(FOX_HEADS, 1), slope_col, gq, gk)


def _flash_chains(chains, k_ref, v_ref, qi, scratch):
    blk = ATT_BLOCK
    n_full = qi
    n_chunk = blk // ROW_CHUNK
    n_tile = blk // PV_TILE
    n_map = len(chains)
    s_refs = [scratch[4 * n] for n in range(n_map)]
    m_refs = [scratch[4 * n + 1] for n in range(n_map)]
    acc_refs = [scratch[4 * n + 2] for n in range(n_map)]
    bmax_refs = [scratch[4 * n + 3] for n in range(n_map)]

    def chunk_max(s_c):
        return jnp.max(s_c.reshape(s_c.shape[0] // SUBLANES, SUBLANES, s_c.shape[-1]), axis=0)

    def park_chunk(n, j, c):
        q_t, k_idx, _ = chains[n]
        start = pl.multiple_of(j * blk + c * ROW_CHUNK, ROW_CHUNK)
        s_c = jnp.dot(k_ref[k_idx, pl.ds(start, ROW_CHUNK), :], q_t, preferred_element_type=_F32)
        s_refs[n][c * ROW_CHUNK:(c + 1) * ROW_CHUNK, :] = s_c
        return chunk_max(s_c)

    def diag_scores(n, c):
        lo = c * DIAG_CHUNK
        s_c = s_refs[n][lo:lo + DIAG_CHUNK, lo:]
        row = lax.broadcasted_iota(jnp.int32, s_c.shape, 0)
        col = lax.broadcasted_iota(jnp.int32, s_c.shape, 1)
        return jnp.where(row <= col, s_c, MASK_VALUE)

    def step(i, maps, diagonal=False, next_maps=()):
        rc = DIAG_CHUNK if diagonal else ROW_CHUNK
        m_new, alpha = {}, {}
        for n in maps:
            bmax = bmax_refs[n][...]
            if diagonal:
                parts = [chunk_max(diag_scores(n, c)) for c in range(blk // rc)]
                bmax = jnp.concatenate(
                    [functools.reduce(jnp.maximum,
                                      [parts[c][:, (b - c) * rc:(b - c + 1) * rc]
                                       for c in range(b + 1)])
                     for b in range(blk // rc)], axis=1)
            m = m_refs[n][...]
            m_new[n] = jnp.maximum(m, jnp.max(bmax, axis=0, keepdims=True))
            alpha[n] = jnp.exp2(m - m_new[n])
        pv_sum = {n: [None] * n_tile for n in maps}
        next_max = {n: None for n in tuple(maps) + tuple(next_maps)}
        p_parts = {n: [] for n in maps}
        for c in range(blk // rc):
            lo = c * rc
            if lo % ROW_CHUNK == 0:
                for n in next_maps:
                    cm = park_chunk(n, 0, lo // ROW_CHUNK)
                    next_max[n] = cm if next_max[n] is None else jnp.maximum(next_max[n], cm)
            for n in maps:
                if diagonal:
                    p_c = jnp.exp2((diag_scores(n, c) - m_new[n][:, lo:]).astype(_BF16))
                    if lo:
                        p_c = jnp.concatenate([jnp.zeros((rc, lo), _BF16), p_c], axis=1)
                else:
                    p_c = jnp.exp2((s_refs[n][lo:lo + rc, :] - m_new[n]).astype(_BF16))
                p_parts[n].append(p_c)
                if not diagonal:
                    cm = park_chunk(n, i + 1, c)
                    next_max[n] = cm if next_max[n] is None else jnp.maximum(next_max[n], cm)
                if (lo + rc) % PV_TILE:
                    continue
                p_rows = jnp.concatenate(p_parts[n], axis=0)
                p_parts[n] = []
                k_first = lo + rc - p_rows.shape[0]
                for k0 in range(k_first, lo + rc, PV_TILE):
                    kt = k0 // PV_TILE
                    p_tile = p_rows[k0 - k_first:k0 - k_first + PV_TILE]
                    v_tile = v_ref[chains[n][2], i * (blk // V_BLOCK) + k0 // V_BLOCK, :,
                                   k0 % V_BLOCK:k0 % V_BLOCK + PV_TILE]
                    for nt in range(kt if diagonal else 0, n_tile):
                        piece = jnp.dot(v_tile, p_tile[:, nt * PV_TILE:(nt + 1) * PV_TILE],
                                        preferred_element_type=_F32)
                        pv_sum[n][nt] = piece if pv_sum[n][nt] is None else pv_sum[n][nt] + piece
        for n in maps:
            acc_refs[n][...] = alpha[n] * acc_refs[n][...] + jnp.concatenate(pv_sum[n], axis=1)
            m_refs[n][...] = m_new[n]
            if not diagonal:
                bmax_refs[n][...] = next_max[n]
        for n in next_maps:
            bmax_refs[n][...] = next_max[n]

    groups = [tuple(range(g, min(g + ATT_LOCKSTEP, n_map))) for g in range(0, n_map, ATT_LOCKSTEP)]
    for n in range(n_map):
        if n in groups[0]:
            bmax_refs[n][...] = functools.reduce(
                jnp.maximum, [park_chunk(n, 0, c) for c in range(n_chunk)])
        m_refs[n][...] = jnp.full(m_refs[n].shape, MASK_VALUE, _F32)
        acc_refs[n][...] = jnp.zeros_like(acc_refs[n])

    for g, grp in enumerate(groups):
        def trip(i, _, grp=grp):
            step(i, grp)
            return 0

        lax.fori_loop(0, n_full, trip, 0)
        step(n_full, grp, diagonal=True, next_maps=groups[g + 1] if g + 1 < len(groups) else ())
    return [acc_refs[n][...] for n in range(n_map)]


def _fox_kernel(q_ref, k_ref, v_ref, o_ref, *scratch):
    n_heads = q_ref.shape[0]
    accs = _flash_chains([(q_ref[hh], hh, hh) for hh in range(n_heads)], k_ref, v_ref,
                         pl.program_id(1), scratch)
    outs = [acc[0:HEAD_DIM] / acc[HEAD_DIM:HEAD_DIM + 1] for acc in accs]
    o_t = jnp.concatenate(outs, axis=0)
    o_ref[...] = o_t.T.astype(o_ref.dtype)


def _diff_kernel(q_ref, k_ref, v_ref, lam_ref, g_ref, o_ref, *scratch):
    n_maps = q_ref.shape[0]
    accs = _flash_chains([(q_ref[n], n, n // 2) for n in range(n_maps)], k_ref, v_ref,
                         pl.program_id(1), scratch)
    outs = [acc[0:DIFF_V_DIM] / acc[DIFF_V_DIM:DIFF_V_DIM + 1] for acc in accs]
    lp = lam_ref[...]
    lam = (jnp.exp(jnp.sum(lp[0:1] * lp[1:2], axis=1, keepdims=True))
           - jnp.exp(jnp.sum(lp[2:3] * lp[3:4], axis=1, keepdims=True)) + LAM_INIT)
    heads = []
    for hd in range(n_maps // 2):
        o_t = outs[2 * hd] - lam * outs[2 * hd + 1]
        heads.append(((o_t * _rms_scale(o_t, 0)) * g_ref[...]) * (1.0 - LAM_INIT))
    o_ref[...] = jnp.concatenate(heads, axis=0).T.astype(o_ref.dtype)


def _attention(q_t, k, v_t, kernel_fn, maps_per_step, v_per_step, extra=(), name=None):
    n_maps, _, seq = q_t.shape
    nvb, v_rows = v_t.shape[1], v_t.shape[2]
    n_groups = n_maps // maps_per_step
    out_cols = maps_per_step * HEAD_DIM
    extra_specs = [pl.BlockSpec(e.shape, lambda g, qi: (0, 0)) for e in extra]
    return pl.pallas_call(
        kernel_fn,
        grid=(n_groups, seq // ATT_BLOCK),
        in_specs=[
            pl.BlockSpec((maps_per_step, QK_FEATS, ATT_BLOCK), lambda g, qi: (g, 0, qi)),
            pl.BlockSpec((maps_per_step, seq, QK_FEATS), lambda g, qi: (g, 0, 0),
                         pipeline_mode=pl.Buffered(1)),
            pl.BlockSpec((v_per_step, nvb, v_rows, V_BLOCK), lambda g, qi: (g, 0, 0, 0),
                         pipeline_mode=pl.Buffered(1)),
        ] + extra_specs,
        out_specs=pl.BlockSpec((ATT_BLOCK, out_cols), lambda g, qi: (qi, g)),
        out_shape=jax.ShapeDtypeStruct((seq, n_groups * out_cols), _BF16),
        scratch_shapes=maps_per_step * [
            pltpu.VMEM((ATT_BLOCK, ATT_BLOCK), _F32),
            pltpu.VMEM((1, ATT_BLOCK), _F32),
            pltpu.VMEM((v_rows, ATT_BLOCK), _F32),
            pltpu.VMEM((SUBLANES, ATT_BLOCK), _F32)],
        compiler_params=pltpu.CompilerParams(
            dimension_semantics=("arbitrary", "arbitrary"), vmem_limit_bytes=VMEM_LIMIT_BYTES),
        name=name,
    )(q_t, k, v_t, *extra)


def _out_kernel(x_ref, fox_ref, diff_ref, wo_ref, wu_ref, wd_ref, g_ref, o_ref):
    g = g_ref[...]
    mix = (jnp.dot(fox_ref[...], wo_ref[0:FOX_WIDTH, :], preferred_element_type=_F32)
           + jnp.dot(diff_ref[...], wo_ref[FOX_WIDTH:, :], preferred_element_type=_F32))
    x1 = x_ref[...] + (mix * _rms_scale(mix, -1)) * g[0:1]
    h = ((x1 * _rms_scale(x1, -1)) * g[1:2]).astype(_BF16)
    d = jnp.zeros_like(x1)
    for f in range(D_FF // FF_CHUNK):
        u = jnp.dot(h, wu_ref[:, f * FF_CHUNK:(f + 1) * FF_CHUNK], preferred_element_type=_F32)
        u = jnp.square(jnp.maximum(u, 0.0)).astype(_BF16)
        d = d + jnp.dot(u, wd_ref[f * FF_CHUNK:(f + 1) * FF_CHUNK, :], preferred_element_type=_F32)
    o_ref[...] = x1 + (d * _rms_scale(d, -1)) * g[2:3]


def _output_stage(x2, fox, diff, w_out, w_up, w_down, gains):
    seq = x2.shape[0]
    rows = OUT_ROWS
    const2 = lambda t: (0, 0)
    resident = functools.partial(pl.BlockSpec, index_map=const2, pipeline_mode=pl.Buffered(1))
    return pl.pallas_call(
        _out_kernel,
        grid=(seq // rows,),
        in_specs=[
            pl.BlockSpec((rows, D_MODEL), lambda t: (t, 0)),
            pl.BlockSpec((rows, FOX_WIDTH), lambda t: (t, 0)),
            pl.BlockSpec((rows, DIFF_WIDTH), lambda t: (t, 0)),
            resident((D_MODEL, D_MODEL)),
            resident((D_MODEL, D_FF)),
            resident((D_FF, D_MODEL)),
            pl.BlockSpec((3, D_MODEL), const2),
        ],
        out_specs=pl.BlockSpec((rows, D_MODEL), lambda t: (t, 0)),
        out_shape=jax.ShapeDtypeStruct((seq, D_MODEL), _F32),
        compiler_params=pltpu.CompilerParams(
            dimension_semantics=("arbitrary",), vmem_limit_bytes=VMEM_LIMIT_BYTES),
        name="out_mlp",
    )(x2, fox, diff, w_out, w_up, w_down, gains)


def kernel(x, g_pre_mix, w_in, b_forget, lambda_q1, lambda_k1, lambda_q2, lambda_k2,
           g_subln, w_out, g_post_mix, g_pre_mlp, w_up, w_down, g_post_mlp):
    batch, seq, _ = x.shape
    assert batch == 1 and w_in.shape[0] == 1
    assert FOX_HEADS == DIFF_MAPS == N_BIAS_MAPS
    assert seq % PROJ_ROWS == 0 and seq % ATT_BLOCK == 0 and seq % OUT_ROWS == 0
    assert PROJ_ROWS % V_BLOCK == 0 and ATT_BLOCK % V_BLOCK == 0
    assert V_BLOCK % PV_TILE == 0 and (PV_TILE % ROW_CHUNK == 0 or ROW_CHUNK % PV_TILE == 0)
    x2 = x.reshape(seq, D_MODEL)
    fq_t, fk, fv_t, dq_t, dk, dv_t = _projection(x2, g_pre_mix[0], w_in[0], b_forget[0])
    fox = _attention(fq_t, fk, fv_t, _fox_kernel, ATT_MAPS, ATT_MAPS, name="fox_attn")
    lam_params = jnp.stack([lambda_q1[0], lambda_k1[0], lambda_q2[0], lambda_k2[0]]).astype(_F32)
    diff = _attention(dq_t, dk, dv_t, _diff_kernel, ATT_MAPS, ATT_MAPS // 2,
                      extra=(lam_params, g_subln[0].reshape(DIFF_V_DIM, 1)), name="diff_attn")
    gains = jnp.stack([g_post_mix[0], g_pre_mlp[0], g_post_mlp[0]])
    out = _output_stage(x2, fox, diff, w_out[0].astype(_BF16), w_up[0].astype(_BF16),
                        w_down[0].astype(_BF16), gains)
    return out.reshape(batch, seq, D_MODEL)
```
